```python
import jax, jax.numpy as jnp
from jax import lax
import numpy as np

D_MODEL = 1024
BATCH = 16
SEQ = 4096
DEPTH = 2
DEC_BATCH = 4
DEC_SEQ = 4096
PAST_LEN = 128

DN_HEADS = 4
DN_HEAD_DIM = 128
DN_WIDTH = DN_HEADS * DN_HEAD_DIM
CONV_WIDTH = 5
CHUNK = 64
ATTN_Q_HEADS = 8
ATTN_KV_HEADS = 2
ATTN_GROUP = ATTN_Q_HEADS // ATTN_KV_HEADS
ATTN_HEAD_DIM = 64
ATTN_WIDTH = ATTN_Q_HEADS * ATTN_HEAD_DIM
KV_WIDTH = ATTN_KV_HEADS * ATTN_HEAD_DIM
WINDOW = 128
BLOCK = 128
D_FF = 2816
N_BRANCHES = 2
NORM_EPS = 1e-6
IN_SPLITS = (DN_WIDTH,) * 4 + (DN_HEADS,) * 4 + (ATTN_WIDTH, KV_WIDTH, KV_WIDTH) + (D_MODEL,) * N_BRANCHES
IN_WIDTH = sum(IN_SPLITS)

kernel_name = 'hybrid_bidir_deltanet_swa_encoder'


def rms_norm(x, g):
    xf = x.astype(jnp.float32)
    y = xf * lax.rsqrt(jnp.mean(xf * xf, axis=-1, keepdims=True) + NORM_EPS)
    return (y * g.astype(jnp.float32)).astype(x.dtype)


def l2_normalize(x):
    xf = x.astype(jnp.float32)
    return xf * lax.rsqrt(jnp.sum(xf * xf, axis=-1, keepdims=True) + NORM_EPS)


def swiglu(x, w_in, w_out):
    gate, up = jnp.split(x @ w_in, 2, axis=-1)
    return (jax.nn.silu(gate) * up) @ w_out


def centred_depthwise_conv(x, w):
    pad = CONV_WIDTH // 2
    return lax.conv_general_dilated(
        x, w[:, None, :].astype(x.dtype), window_strides=(1,), padding=[(pad, pad)],
        dimension_numbers=('NWC', 'WIO', 'NWC'), feature_group_count=x.shape[-1])


def chunked_gated_delta(q, k, v, g, beta):
    B, T, H, Dk = q.shape
    Dv = v.shape[-1]
    N = T // CHUNK

    def to_chunks(t):
        return jnp.moveaxis(t.reshape((B, N, CHUNK, H) + t.shape[3:]), 3, 1)

    q = to_chunks(q) * (Dk ** -0.5)
    k = to_chunks(k)
    v = to_chunks(v)
    beta = to_chunks(beta)
    gc = jnp.cumsum(to_chunks(g), axis=-1)

    idx = jnp.arange(CHUNK)
    causal = idx[:, None] >= idx[None, :]
    strict = idx[:, None] > idx[None, :]
    diff = gc[..., :, None] - gc[..., None, :]
    decay = jnp.where(causal, jnp.exp(jnp.where(causal, diff, 0.0)), 0.0)

    kb = k * beta[..., None]
    lower = jnp.where(strict, jnp.einsum('bhncd,bhnsd->bhncs', kb, k) * decay, 0.0)
    a_mat = lower + jnp.eye(CHUNK, dtype=jnp.float32)
    rhs = jnp.concatenate([v * beta[..., None], kb * jnp.exp(gc)[..., None]], axis=-1)
    sol = lax.linalg.triangular_solve(a_mat, rhs, left_side=True, lower=True, unit_diagonal=True)
    u, w = sol[..., :Dv], sol[..., Dv:]

    intra = jnp.where(causal, jnp.einsum('bhncd,bhnsd->bhncs', q, k) * decay, 0.0)
    q_dec = q * jnp.exp(gc)[..., None]
    k_dec = k * jnp.exp(gc[..., -1:] - gc)[..., None]
    chunk_decay = jnp.exp(gc[..., -1])

    def step(S, xs):
        u_c, w_c, qd_c, kd_c, intra_c, dec_c = xs
        v_new = u_c - jnp.einsum('bhcd,bhde->bhce', w_c, S)
        o_c = jnp.einsum('bhcd,bhde->bhce', qd_c, S) + jnp.einsum('bhcs,bhse->bhce', intra_c, v_new)
        S = S * dec_c[..., None, None] + jnp.einsum('bhcd,bhce->bhde', kd_c, v_new)
        return S, o_c

    xs = tuple(jnp.moveaxis(t, 2, 0) for t in (u, w, q_dec, k_dec, intra, chunk_decay))
    S0 = jnp.zeros((B, H, Dk, Dv), jnp.float32)
    _, o = lax.scan(step, S0, xs)
    return o.transpose(1, 0, 3, 2, 4).reshape(B, T, H, Dv)


def deltanet_branch(q, k, v, z, b_f, b_b, a_f, a_b, conv_w, a_log, dt_bias, norm_w):
    B, T, _ = q.shape
    f32 = jnp.float32
    qkv = jax.nn.silu(centred_depthwise_conv(jnp.concatenate([q, k, v], axis=-1), conv_w))
    q, k, v = jnp.split(qkv, 3, axis=-1)
    heads = lambda t: t.reshape(B, T, DN_HEADS, DN_HEAD_DIM)
    q = l2_normalize(heads(q))
    k = l2_normalize(heads(k))
    v = heads(v).astype(f32)

    def log_decay(a, d):
        return -jnp.exp(a_log[d].astype(f32)) * jax.nn.softplus(a.astype(f32) + dt_bias[d].astype(f32))

    g_f, g_b = log_decay(a_f, 0), log_decay(a_b, 1)
    beta_f, beta_b = jax.nn.sigmoid(b_f.astype(f32)), jax.nn.sigmoid(b_b.astype(f32))
    rev = lambda t: jnp.flip(t, axis=1)
    o_f = chunked_gated_delta(q, k, v, g_f, beta_f)
    o_b = rev(chunked_gated_delta(rev(q), rev(k), rev(v), rev(g_b), rev(beta_b)))
    o = rms_norm(o_f + o_b, norm_w) * jax.nn.silu(heads(z).astype(f32))
    return o.reshape(B, T, DN_WIDTH).astype(z.dtype)


def alibi_slopes(n_heads):
    return jnp.exp2(-8.0 * jnp.arange(1, n_heads + 1, dtype=jnp.float32) / n_heads)


def window_attention_branch(q, k, v, sink):
    B, T, _ = q.shape
    nb = T // BLOCK
    q = q.reshape(B, T, ATTN_KV_HEADS, ATTN_GROUP, ATTN_HEAD_DIM) * (ATTN_HEAD_DIM ** -0.5)
    k = k.reshape(B, T, ATTN_KV_HEADS, ATTN_HEAD_DIM)
    v = v.reshape(B, T, ATTN_KV_HEADS, ATTN_HEAD_DIM)
    kp = jnp.pad(k, ((0, 0), (BLOCK, BLOCK), (0, 0), (0, 0)))
    vp = jnp.pad(v, ((0, 0), (BLOCK, BLOCK), (0, 0), (0, 0)))

    a = jnp.arange(BLOCK)
    j = jnp.arange(3 * BLOCK)
    rel = j[None, :] - BLOCK - a[:, None]
    band = jnp.abs(rel) <= WINDOW
    slopes = alibi_slopes(ATTN_Q_HEADS).reshape(ATTN_KV_HEADS, ATTN_GROUP, 1, 1)
    alibi = -slopes * jnp.abs(rel).astype(jnp.float32)
    sink_l = sink.astype(jnp.float32).reshape(1, ATTN_KV_HEADS, ATTN_GROUP, 1, 1)

    def block(i):
        qb = lax.dynamic_slice_in_dim(q, i * BLOCK, BLOCK, axis=1)
        kb = lax.dynamic_slice_in_dim(kp, i * BLOCK, 3 * BLOCK, axis=1)
        vb = lax.dynamic_slice_in_dim(vp, i * BLOCK, 3 * BLOCK, axis=1)
        s_pos = i * BLOCK - BLOCK + j
        valid = band & ((s_pos >= 0) & (s_pos < T))[None, :]
        s = jnp.einsum('bqhgd,bkhd->bhgqk', qb, kb).astype(jnp.float32) + alibi
        s = jnp.where(valid, s, -jnp.inf)
        m = jnp.maximum(jnp.max(s, axis=-1, keepdims=True), sink_l)
        p = jnp.exp(s - m)
        p = p / (jnp.sum(p, axis=-1, keepdims=True) + jnp.exp(sink_l - m))
        return jnp.einsum('bhgqk,bkhd->bqhgd', p.astype(vb.dtype), vb)

    o = lax.map(block, jnp.arange(nb))
    return jnp.moveaxis(o, 0, 1).reshape(B, T, ATTN_WIDTH)


def token_mixer(u, w_in, conv_w, a_log, dt_bias, dn_norm_w, attn_sink, w_branch_a, w_branch_b, w_out):
    points = np.cumsum(IN_SPLITS)[:-1].tolist()
    (dq, dk, dv, dz, b_f, b_b, a_f, a_b, aq, ak, av, gate_a, gate_b) = jnp.split(u @ w_in, points, axis=-1)
    out_a = deltanet_branch(dq, dk, dv, dz, b_f, b_b, a_f, a_b, conv_w, a_log, dt_bias, dn_norm_w)
    out_b = window_attention_branch(aq, ak, av, attn_sink)
    merged = jax.nn.sigmoid(gate_a) * (out_a @ w_branch_a) + jax.nn.sigmoid(gate_b) * (out_b @ w_branch_b)
    return merged @ w_out


def encoder_trunk(x, params):
    (f1_pre, f1_post, f1_w_in, f1_w_out, m_pre, m_post, w_in, conv_w, a_log, dt_bias, dn_norm_w,
     attn_sink, w_branch_a, w_branch_b, w_out, f2_pre, f2_post, f2_w_in, f2_w_out) = params
    for l in range(DEPTH):
        x = x + 0.5 * rms_norm(swiglu(rms_norm(x, f1_pre[l]), f1_w_in[l], f1_w_out[l]), f1_post[l])
        mix = token_mixer(rms_norm(x, m_pre[l]), w_in[l], conv_w[l], a_log[l], dt_bias[l], dn_norm_w[l],
                          attn_sink[l], w_branch_a[l], w_branch_b[l], w_out[l])
        x = x + rms_norm(mix, m_post[l])
        x = x + 0.5 * rms_norm(swiglu(rms_norm(x, f2_pre[l]), f2_w_in[l], f2_w_out[l]), f2_post[l])
    return x


def setup_inputs(seed: int = 0) -> dict:
    key = jax.random.key(seed)
    ks = jax.random.split(key, 24)
    L, D = DEPTH, D_MODEL
    nrm = lambda k, shape, fan_in: jax.random.normal(k, shape, jnp.float32) * (fan_in ** -0.5)
    gain = lambda k, shape: 1.0 + 0.05 * jax.random.normal(k, shape, jnp.float32)
    dt = jnp.exp(jax.random.uniform(ks[11], (L, 2, DN_HEADS), jnp.float32, minval=-6.907755, maxval=-2.302585))
    return {
        'x_prompt': jax.random.normal(ks[0], (BATCH, SEQ, D), jnp.float32),
        'x_sample': jax.random.normal(ks[1], (DEC_BATCH, DEC_SEQ, D), jnp.float32),
        'ffn1_norm_pre': gain(ks[2], (L, D)),
        'ffn1_norm_post': gain(ks[3], (L, D)),
        'ffn1_w_in': nrm(ks[4], (L, D, 2 * D_FF), D),
        'ffn1_w_out': nrm(ks[5], (L, D_FF, D), D_FF),
        'mix_norm_pre': gain(ks[6], (L, D)),
        'mix_norm_post': gain(ks[7], (L, D)),
        'mix_w_in': nrm(ks[8], (L, D, IN_WIDTH), D),
        'dn_conv_w': nrm(ks[9], (L, CONV_WIDTH, 3 * DN_WIDTH), CONV_WIDTH),
        'dn_a_log': jnp.log(jax.random.uniform(ks[10], (L, 2, DN_HEADS), jnp.float32, minval=1.0, maxval=16.0)),
        'dn_dt_bias': dt + jnp.log(-jnp.expm1(-dt)),
        'dn_norm_w': gain(ks[12], (L, DN_HEAD_DIM)),
        'attn_sink': 0.5 * jax.random.normal(ks[13], (L, ATTN_Q_HEADS), jnp.float32),
        'w_branch_a': nrm(ks[14], (L, DN_WIDTH, D), DN_WIDTH),
        'w_branch_b': nrm(ks[15], (L, ATTN_WIDTH, D), ATTN_WIDTH),
        'mix_w_out': nrm(ks[16], (L, D, D), D),
        'ffn2_norm_pre': gain(ks[17], (L, D)),
        'ffn2_norm_post': gain(ks[18], (L, D)),
        'ffn2_w_in': nrm(ks[19], (L, D, 2 * D_FF), D),
        'ffn2_w_out': nrm(ks[20], (L, D_FF, D), D_FF),
    }


def reference(x_prompt, x_sample, ffn1_norm_pre, ffn1_norm_post, ffn1_w_in, ffn1_w_out,
              mix_norm_pre, mix_norm_post, mix_w_in, dn_conv_w, dn_a_log, dn_dt_bias, dn_norm_w,
              attn_sink, w_branch_a, w_branch_b, mix_w_out,
              ffn2_norm_pre, ffn2_norm_post, ffn2_w_in, ffn2_w_out):
    params = (ffn1_norm_pre, ffn1_norm_post, ffn1_w_in, ffn1_w_out,
              mix_norm_pre, mix_norm_post, mix_w_in, dn_conv_w, dn_a_log, dn_dt_bias, dn_norm_w,
              attn_sink, w_branch_a, w_branch_b, mix_w_out,
              ffn2_norm_pre, ffn2_norm_post, ffn2_w_in, ffn2_w_out)
    y_prompt = encoder_trunk(x_prompt, params)
    y_sample = encoder_trunk(x_sample, params)
    return (y_prompt, y_sample)
```

```python
import functools

import jax
import jax.numpy as jnp
from jax import lax
from jax.experimental import pallas as pl
from jax.experimental.pallas import tpu as pltpu

F32 = jnp.float32
BF16 = jnp.bfloat16
HIGHEST = lax.Precision.HIGHEST

NORM_EPS = 1e-6
DN_HEADS = 4
DN_HEAD_DIM = 128
DN_WIDTH = DN_HEADS * DN_HEAD_DIM
CONV_WIDTH = 5
CHUNK = 64
ATTN_Q_HEADS = 8
ATTN_KV_HEADS = 2
ATTN_GROUP = ATTN_Q_HEADS // ATTN_KV_HEADS
ATTN_HEAD_DIM = 64
ATTN_WIDTH = ATTN_Q_HEADS * ATTN_HEAD_DIM
KV_WIDTH = ATTN_KV_HEADS * ATTN_HEAD_DIM
WINDOW = 128

V7X_LANES = 128
V7X_SUBLANES = 8
V7X_VMEM_LIMIT_BYTES = 56 * 1024 * 1024

SUPER = 2 * CHUNK
assert SUPER == V7X_LANES and WINDOW == V7X_LANES and DN_HEAD_DIM == V7X_LANES
MASKED = -1e30

_EYE, _BONES, _TRI0, _SREL0, _M8_0, _E8_0, _E16_0, _E32_0 = 0, 1, 2, 4, 6, 8, 10, 12
_NMASK = 14


def _rms(x, g):
    return x * lax.rsqrt(jnp.mean(x * x, axis=-1, keepdims=True) + NORM_EPS) * g


def _dot(a, b):
    return jnp.dot(a, b, preferred_element_type=F32)


def _dot_nt(a, b):
    return lax.dot_general(a, b, (((1,), (1,)), ((), ())), preferred_element_type=F32)


def _dot_tn(a, b):
    return lax.dot_general(a, b, (((0,), (0,)), ((), ())), preferred_element_type=F32)


def _softplus(x):
    return jnp.maximum(x, 0.0) + jnp.log1p(jnp.exp(-jnp.abs(x)))


def _resident(shape):
    nd = len(shape)
    return pl.BlockSpec(shape, lambda *_: (0,) * nd, pipeline_mode=pl.Buffered(1))


def _params(sem):
    return pltpu.CompilerParams(dimension_semantics=sem, vmem_limit_bytes=V7X_VMEM_LIMIT_BYTES)


def _ffn_body(x_ref, gpre_ref, gpost_ref, win_ref, wout_ref, o_ref, *, d_ff):
    x = x_ref[...]
    h = _rms(x, gpre_ref[...]).astype(BF16)
    gu = _dot(h, win_ref[...])
    gate = gu[:, :d_ff]
    up = gu[:, d_ff:]
    act = (gate * jax.nn.sigmoid(gate) * up).astype(BF16)
    y = _dot(act, wout_ref[...])
    o_ref[...] = x + 0.5 * _rms(y, gpost_ref[...])


def _ffn(x, gpre, gpost, w_in, w_out, tm):
    B, T, D = x.shape
    d_ff = w_out.shape[0]
    x2 = x.reshape(B * T, D)
    out = pl.pallas_call(
        functools.partial(_ffn_body, d_ff=d_ff),
        grid=(B * T // tm,),
        in_specs=[
            pl.BlockSpec((tm, D), lambda i: (i, 0)),
            _resident((1, D)),
            _resident((1, D)),
            _resident((D, 2 * d_ff)),
            _resident((d_ff, D)),
        ],
        out_specs=pl.BlockSpec((tm, D), lambda i: (i, 0)),
        out_shape=jax.ShapeDtypeStruct((B * T, D), F32),
        compiler_params=_params(("arbitrary",)),
        name="ffn",
    )(x2, gpre, gpost, w_in, w_out)
    return out.reshape(B, T, D)


def _proj_body(x_ref, g_ref, wdn_ref, wsc_ref, wsr_ref, wat_ref, wg_ref,
               odn_ref, osc_ref, osr_ref, oat_ref, og_ref):
    u = _rms(x_ref[...], g_ref[...]).astype(BF16)
    odn_ref[...] = _dot(u, wdn_ref[...]).astype(odn_ref.dtype)
    osc_ref[...] = _dot(u, wsc_ref[...])
    osr_ref[...] = _dot_nt(wsr_ref[...], u)
    oat_ref[...] = _dot(u, wat_ref[...]).astype(oat_ref.dtype)
    og_ref[...] = _dot(u, wg_ref[...]).astype(og_ref.dtype)


def _proj(x, g, wdn, wsc, wsr, wat, wg, tm):
    B, T, D = x.shape
    tok = lambda n: pl.BlockSpec((None, tm, n), lambda b, i: (b, i, 0))
    nsr = wsr.shape[0]
    return pl.pallas_call(
        _proj_body,
        grid=(B, T // tm),
        in_specs=[tok(D), _resident((1, D)), _resident(wdn.shape), _resident(wsc.shape),
                  _resident(wsr.shape), _resident(wat.shape), _resident(wg.shape)],
        out_specs=[tok(wdn.shape[1]), tok(wsc.shape[1]),
                   pl.BlockSpec((None, nsr, tm), lambda b, i: (b, 0, i)),
                   tok(wat.shape[1]), tok(wg.shape[1])],
        out_shape=[
            jax.ShapeDtypeStruct((B, T, wdn.shape[1]), BF16),
            jax.ShapeDtypeStruct((B, T, wsc.shape[1]), F32),
            jax.ShapeDtypeStruct((B, nsr, T), F32),
            jax.ShapeDtypeStruct((B, T, wat.shape[1]), BF16),
            jax.ShapeDtypeStruct((B, T, wg.shape[1]), BF16),
        ],
        compiler_params=_params(("arbitrary", "arbitrary")),
        name="proj",
    )(x, g, wdn, wsc, wsr, wat, wg)


def _build_masks(msk):
    ii = lax.broadcasted_iota(jnp.int32, (SUPER, SUPER), 0)
    jj = lax.broadcasted_iota(jnp.int32, (SUPER, SUPER), 1)
    same = lambda s: jnp.right_shift(ii, s) == jnp.right_shift(jj, s)
    f = lambda c: jnp.where(c, 1.0, 0.0).astype(F32)
    msk[_EYE] = f(ii == jj)
    msk[_BONES] = f(same(6))
    for d in range(2):
        after = (ii > jj) if d == 0 else (ii < jj)
        after_eq = (ii >= jj) if d == 0 else (ii <= jj)
        msk[_TRI0 + d] = f(same(6) & after_eq)
        msk[_SREL0 + d] = f(same(6) & after)
        msk[_M8_0 + d] = f(same(3) & after)
        msk[_E8_0 + d] = f(same(4) & jnp.logical_not(same(3)) & after)
        msk[_E16_0 + d] = f(same(5) & jnp.logical_not(same(4)) & after)
        msk[_E32_0 + d] = f(same(6) & jnp.logical_not(same(5)) & after)


def _unit_tri_inverse(lm, msk, d):
    mm = lambda a, b: _dot(a.astype(BF16), b.astype(BF16))
    x = lm * msk[_M8_0 + d]
    x2 = mm(x, x)
    x3 = mm(x2, x)
    x4 = mm(x2, x2)
    p1 = msk[_EYE] - x + x2 - x3
    inv = p1 + mm(p1, x4)
    for lvl in (_E8_0, _E16_0, _E32_0):
        e = lm * msk[lvl + d]
        inv = inv - mm(mm(inv, e), inv)
    return inv


def _dn_superchunk(d, r0, qn, kn, vn, gcol, grow, msk, s_ref, oacc):
    rows = pl.ds(r0, SUPER)
    q = qn[rows, :]
    k = kn[rows, :]
    kb = k.astype(BF16)
    vb = vn[rows, :].astype(BF16)
    qg = _dot_nt(jnp.concatenate([q.astype(BF16), kb], axis=0), kb)
    qk = qg[:SUPER]
    kk = qg[SUPER:]
    gcv = gcol[rows, :]
    grv = grow[:, rows]
    cs = jnp.dot(jnp.concatenate([msk[_TRI0 + d], msk[_BONES]], axis=0), gcv,
                 precision=HIGHEST, preferred_element_type=F32)
    gc_c = cs[:SUPER, 2 + d:3 + d]
    tot_c = cs[SUPER:, 2 + d:3 + d]
    beta_c = gcv[:, d:d + 1]
    gr = jnp.dot(grv, msk[_TRI0 + 1 - d], precision=HIGHEST, preferred_element_type=F32)
    gc_r = gr[2 + d:3 + d, :]
    beta_r = grv[d:d + 1, :]

    srel = msk[_SREL0 + d]
    dec = jnp.exp((gc_c - gc_r) * srel) * srel
    lm = kk * beta_c * dec
    intra = (qk * (dec + msk[_EYE])).astype(BF16)
    tinv = _unit_tri_inverse(lm, msk, d)
    tb = tinv * beta_r
    tbe = tb * jnp.exp(gc_r)
    u = _dot(tb.astype(BF16), vb)
    w = _dot(tbe.astype(BF16), kb)
    wu = jnp.concatenate([w, u], axis=1).astype(BF16)
    iw = _dot(intra, wu)
    qp = (q * jnp.exp(gc_c) - iw[:, :DN_HEAD_DIM]).astype(BF16)
    o0 = iw[:, DN_HEAD_DIM:]
    kdec = (k * jnp.exp(tot_c - gc_c)).astype(BF16)
    etot = jnp.exp(tot_c)

    s = s_ref[...]
    outs = [None, None]
    for c in ((0, 1) if d == 0 else (1, 0)):
        sl = slice(c * CHUNK, (c + 1) * CHUNK)
        mn = _dot_tn(kdec[sl], wu[sl])
        a = jnp.concatenate([mn[:, :DN_HEAD_DIM].astype(BF16), qp[sl]], axis=0)
        r = _dot(a, s.astype(BF16))
        outs[c] = r[DN_HEAD_DIM:] + o0[sl]
        e = etot[c * CHUNK:c * CHUNK + V7X_SUBLANES]
        s3 = s.reshape(DN_HEAD_DIM // V7X_SUBLANES, V7X_SUBLANES, DN_HEAD_DIM) * e[None]
        s = s3.reshape(DN_HEAD_DIM, DN_HEAD_DIM) - r[:DN_HEAD_DIM] + mn[:, DN_HEAD_DIM:]
    s_ref[...] = s
    oacc[rows, :] += jnp.concatenate(outs, axis=0)


def _dn_body(q_ref, k_ref, v_ref, z_ref, gc_ref, gr_ref, cwq_ref, cwk_ref, cwv_ref,
             lc_ref, lr_ref, nw_ref, o_ref,
             pad, qn, kn, vn, gcol, grow, oacc, s_f, s_b, msk, *, T, rt):
    nt = T // rt
    ns = T // SUPER
    halo = V7X_SUBLANES
    _build_masks(msk)

    lc = lc_ref[...]
    lane = lax.broadcasted_iota(jnp.int32, (rt, V7X_LANES), 1)

    def gate_tile(i, _):
        r = pl.ds(pl.multiple_of(i * rt, rt), rt)
        x = gc_ref[r, :]
        g = -jnp.exp(lc[0:1, :]) * _softplus(x + lc[1:2, :])
        gcol[r, :] = jnp.where(lane < 2, jax.nn.sigmoid(x), g)
        oacc[r, :] = jnp.zeros((rt, DN_HEAD_DIM), F32)
        return 0

    lax.fori_loop(0, nt, gate_tile, 0)
    xr = gr_ref[...]
    lr = lr_ref[...]
    rowi = lax.broadcasted_iota(jnp.int32, xr.shape, 0)
    grow[...] = jnp.where(rowi < 2, jax.nn.sigmoid(xr),
                          -jnp.exp(lr[:, 0:1]) * _softplus(xr + lr[:, 1:2]))

    pad[0:halo, :] = jnp.zeros((halo, DN_HEAD_DIM), F32)
    pad[T + halo:T + 2 * halo, :] = jnp.zeros((halo, DN_HEAD_DIM), F32)
    for src, cw_ref, dst, scale in ((q_ref, cwq_ref, qn, DN_HEAD_DIM ** -0.5),
                                    (k_ref, cwk_ref, kn, 1.0), (v_ref, cwv_ref, vn, None)):
        def fill(i, _, src=src):
            pad[pl.ds(pl.multiple_of(i * rt, rt) + halo, rt), :] = \
                src[pl.ds(pl.multiple_of(i * rt, rt), rt), :].astype(F32)
            return 0

        lax.fori_loop(0, nt, fill, 0)
        cw = cw_ref[...]

        def conv_tile(i, _, dst=dst, scale=scale, cw=cw):
            base = i * rt + halo - CONV_WIDTH // 2
            acc = pad[pl.ds(base, rt), :] * cw[0:1, :]
            for j in range(1, CONV_WIDTH):
                acc = acc + pad[pl.ds(base + j, rt), :] * cw[j:j + 1, :]
            y = acc * jax.nn.sigmoid(acc)
            if scale is not None:
                y = y * (lax.rsqrt(jnp.sum(y * y, axis=-1, keepdims=True) + NORM_EPS) * scale)
            dst[pl.ds(pl.multiple_of(i * rt, rt), rt), :] = y
            return 0

        lax.fori_loop(0, nt, conv_tile, 0)

    s_f[...] = jnp.zeros((DN_HEAD_DIM, DN_HEAD_DIM), F32)
    s_b[...] = jnp.zeros((DN_HEAD_DIM, DN_HEAD_DIM), F32)

    def scan_step(i, _):
        _dn_superchunk(0, pl.multiple_of(i * SUPER, SUPER), qn, kn, vn, gcol, grow, msk, s_f, oacc)
        _dn_superchunk(1, pl.multiple_of((ns - 1 - i) * SUPER, SUPER), qn, kn, vn, gcol, grow, msk, s_b, oacc)
        return 0

    lax.fori_loop(0, ns, scan_step, 0)

    nw = nw_ref[...]

    def out_tile(i, _):
        r = pl.ds(pl.multiple_of(i * rt, rt), rt)
        z = z_ref[r, :].astype(F32)
        o_ref[r, :] = (_rms(oacc[r, :], nw) * (z * jax.nn.sigmoid(z))).astype(o_ref.dtype)
        return 0

    lax.fori_loop(0, nt, out_tile, 0)


def _deltanet(y_dn, y_sc, y_sr, conv_w, lc, lr, nw):
    B, T, _ = y_dn.shape
    H, hd = DN_HEADS, DN_HEAD_DIM
    rt = min(256, T)
    col = lambda off: pl.BlockSpec((None, T, hd), lambda b, h: (b, 0, off + h))
    cw = lambda off: pl.BlockSpec((CONV_WIDTH, hd), lambda b, h: (0, off + h))
    tok = pltpu.VMEM((T, hd), F32)
    return pl.pallas_call(
        functools.partial(_dn_body, T=T, rt=rt),
        grid=(B, H),
        in_specs=[col(0), col(H), col(2 * H), col(3 * H),
                  pl.BlockSpec((None, T, V7X_LANES), lambda b, h: (b, 0, h)),
                  pl.BlockSpec((None, V7X_SUBLANES, T), lambda b, h: (b, h, 0)),
                  cw(0), cw(H), cw(2 * H),
                  pl.BlockSpec((None, 2, V7X_LANES), lambda b, h: (h, 0, 0)),
                  pl.BlockSpec((None, V7X_SUBLANES, 2), lambda b, h: (h, 0, 0)),
                  pl.BlockSpec((1, hd), lambda b, h: (0, 0))],
        out_specs=pl.BlockSpec((None, T, hd), lambda b, h: (b, 0, h)),
        out_shape=jax.ShapeDtypeStruct((B, T, DN_WIDTH), BF16),
        scratch_shapes=[pltpu.VMEM((T + 2 * V7X_SUBLANES, hd), F32), tok, tok, tok,
                        pltpu.VMEM((T, V7X_LANES), F32), pltpu.VMEM((V7X_SUBLANES, T), F32), tok,
                        pltpu.VMEM((hd, hd), F32), pltpu.VMEM((hd, hd), F32),
                        pltpu.VMEM((_NMASK, SUPER, SUPER), F32)],
        compiler_params=_params(("arbitrary", "arbitrary")),
        name="deltanet",
    )(y_dn, y_dn, y_dn, y_dn, y_sc, y_sr, conv_w, conv_w, conv_w, lc, lr, nw)


def _attn_body(sink_ref, q_ref, k_ref, v_ref, o_ref, k2, v2, bias, *, T, tq, rt):
    i = pl.program_id(1)
    nb = T // WINDOW
    half = ATTN_HEAD_DIM
    lane = lax.broadcasted_iota(jnp.int32, (WINDOW, V7X_LANES), 1)
    lo_mask = jnp.where(lane < half, 1.0, 0.0).astype(F32)
    hi_mask = 1.0 - lo_mask

    @pl.when(i == 0)
    def _init():
        zeros = jnp.zeros((WINDOW, V7X_LANES), BF16)
        for kv in range(ATTN_KV_HEADS):
            for r0 in (0, T + WINDOW):
                k2[kv, r0:r0 + WINDOW, :] = zeros
                v2[kv, r0:r0 + WINDOW, 0:V7X_LANES] = zeros
                v2[kv, r0:r0 + WINDOW, V7X_LANES:2 * V7X_LANES] = jnp.ones((WINDOW, V7X_LANES), BF16)
        lane_r = lax.broadcasted_iota(jnp.int32, (rt, V7X_LANES), 1)

        def fill(t, _):
            src = pl.ds(pl.multiple_of(t * rt, rt), rt)
            dst = pl.ds(pl.multiple_of(t * rt, rt) + WINDOW, rt)
            kf = k_ref[src, :].astype(F32)
            vf = v_ref[src, :].astype(F32)
            for kv in range(ATTN_KV_HEADS):
                sel = (lane_r < half) if kv == 0 else (lane_r >= half)
                km = jnp.where(sel, kf, 0.0)
                vm = jnp.where(sel, vf, 0.0)
                k2[kv, dst, :] = (km + pltpu.roll(km, half, 1)).astype(BF16)
                v2[kv, dst, 0:V7X_LANES] = (vm + pltpu.roll(vm, half, 1)).astype(BF16)
                v2[kv, dst, V7X_LANES:2 * V7X_LANES] = jnp.ones((rt, V7X_LANES), BF16)
            return 0

        lax.fori_loop(0, T // rt, fill, 0)
        a = lax.broadcasted_iota(jnp.int32, (WINDOW, 3 * WINDOW), 0)
        j = lax.broadcasted_iota(jnp.int32, (WINDOW, 3 * WINDOW), 1)
        dist = jnp.abs(j - WINDOW - a)
        for h in range(ATTN_Q_HEADS):
            slope = 2.0 ** (-8.0 * (h + 1) / ATTN_Q_HEADS)
            bias[h] = jnp.where(dist <= WINDOW, -slope * dist.astype(F32), MASKED)

    jrow = lax.broadcasted_iota(jnp.int32, (1, 3 * WINDOW), 1)

    def block(qi, _):
        gi = i * (tq // WINDOW) + qi
        qr = pl.ds(pl.multiple_of(qi * WINDOW, WINDOW), WINDOW)
        win = pl.ds(pl.multiple_of(gi * WINDOW, WINDOW), 3 * WINDOW)
        first_valid = jnp.where(gi == 0, WINDOW, 0)
        end_valid = jnp.where(gi == nb - 1, 2 * WINDOW, 3 * WINDOW)
        outside = (jrow < first_valid) | (jrow >= end_valid)
        pos_bias = jnp.where(outside, MASKED, 0.0)
        for kv in range(ATTN_KV_HEADS):
            kw = k2[kv, win, :]
            vw = v2[kv, win, :]
            for pair in range(ATTN_GROUP // 2):
                c0 = (kv * (ATTN_GROUP // 2) + pair) * V7X_LANES
                qf = q_ref[qr, c0:c0 + V7X_LANES].astype(F32) * (ATTN_HEAD_DIM ** -0.5)
                tiles = []
                for hh, hmask in enumerate((lo_mask, hi_mask)):
                    h = kv * ATTN_GROUP + pair * 2 + hh
                    sink = sink_ref[h]
                    s = _dot_nt((qf * hmask).astype(BF16), kw) + bias[h] + pos_bias
                    m = jnp.maximum(jnp.max(s, axis=-1, keepdims=True), sink)
                    p = jnp.exp(s - m).astype(BF16)
                    ov = _dot(p, vw)
                    den = ov[:, V7X_LANES:] + jnp.exp(sink - m)
                    tiles.append(ov[:, :V7X_LANES] / den)
                o_ref[qr, c0:c0 + V7X_LANES] = jnp.where(lane < half, tiles[0], tiles[1]).astype(o_ref.dtype)
        return 0

    lax.fori_loop(0, tq // WINDOW, block, 0)


def _attention(y_at, sink, tq):
    B, T, _ = y_at.shape
    rt = min(256, T)
    return pl.pallas_call(
        functools.partial(_attn_body, T=T, tq=tq, rt=rt),
        grid=(B, T // tq),
        in_specs=[pl.BlockSpec(memory_space=pltpu.SMEM),
                  pl.BlockSpec((None, tq, ATTN_WIDTH), lambda b, i: (b, i, 0)),
                  pl.BlockSpec((None, T, KV_WIDTH), lambda b, i: (b, 0, ATTN_WIDTH // KV_WIDTH)),
                  pl.BlockSpec((None, T, KV_WIDTH), lambda b, i: (b, 0, ATTN_WIDTH // KV_WIDTH + 1))],
        out_specs=pl.BlockSpec((None, tq, ATTN_WIDTH), lambda b, i: (b, i, 0)),
        out_shape=jax.ShapeDtypeStruct((B, T, ATTN_WIDTH), BF16),
        scratch_shapes=[pltpu.VMEM((ATTN_KV_HEADS, T + 2 * WINDOW, V7X_LANES), BF16),
                        pltpu.VMEM((ATTN_KV_HEADS, T + 2 * WINDOW, 2 * V7X_LANES), BF16),
                        pltpu.VMEM((ATTN_Q_HEADS, WINDOW, 3 * WINDOW), F32)],
        compiler_params=_params(("arbitrary", "arbitrary")),
        name="window_attn",
    )(sink, y_at, y_at, y_at)


def _merge_body(x_ref, a_ref, b_ref, g_ref, pa_ref, pb_ref, wo_ref, gpost_ref, o_ref, *, d_model):
    ya = _dot(a_ref[...], pa_ref[...])
    yb = _dot(b_ref[...], pb_ref[...])
    g = g_ref[...].astype(F32)
    merged = jax.nn.sigmoid(g[:, :d_model]) * ya + jax.nn.sigmoid(g[:, d_model:]) * yb
    mix = _dot(merged.astype(BF16), wo_ref[...])
    o_ref[...] = x_ref[...] + _rms(mix, gpost_ref[...])


def _merge(x, out_a, out_b, y_g, pa, pb, wo, gpost, tm):
    B, T, D = x.shape
    tok = lambda n: pl.BlockSpec((None, tm, n), lambda b, i: (b, i, 0))
    return pl.pallas_call(
        functools.partial(_merge_body, d_model=D),
        grid=(B, T // tm),
        in_specs=[tok(D), tok(out_a.shape[-1]), tok(out_b.shape[-1]), tok(y_g.shape[-1]),
                  _resident(pa.shape), _resident(pb.shape), _resident(wo.shape), _resident((1, D))],
        out_specs=tok(D),
        out_shape=jax.ShapeDtypeStruct((B, T, D), F32),
        compiler_params=_params(("arbitrary", "arbitrary")),
        name="merge",
    )(x, out_a, out_b, y_g, pa, pb, wo, gpost)


def _split_w_in(w_in):
    D = w_in.shape[0]
    H = DN_HEADS
    o_small = 4 * DN_WIDTH
    o_attn = o_small + 4 * H
    o_gate = o_attn + ATTN_WIDTH + 2 * KV_WIDTH
    wdn = w_in[:, :o_small].astype(BF16)
    small = w_in[:, o_small:o_attn].reshape(D, 4, H)
    small = jnp.transpose(small, (0, 2, 1))
    wsc = jnp.pad(small, ((0, 0), (0, 0), (0, V7X_LANES - 4))).reshape(D, H * V7X_LANES).astype(BF16)
    wsr = jnp.pad(small, ((0, 0), (0, 0), (0, V7X_SUBLANES - 4))).reshape(D, H * V7X_SUBLANES).T.astype(BF16)
    wat = w_in[:, o_attn:o_gate].astype(BF16)
    wg = w_in[:, o_gate:].astype(BF16)
    return wdn, wsc, wsr, wat, wg


def _gate_consts(a_log, dt_bias):
    H = DN_HEADS
    both = jnp.stack([a_log, dt_bias], axis=0).astype(F32)
    per_head = jnp.transpose(both, (2, 0, 1))
    lc = jnp.zeros((H, 2, V7X_LANES), F32).at[:, :, 2:4].set(per_head)
    lr = jnp.zeros((H, V7X_SUBLANES, 2), F32).at[:, 2:4, :].set(jnp.transpose(per_head, (0, 2, 1)))
    return lc, lr


def _tile(n, pref):
    t = min(pref, n)
    assert n % t == 0
    return t


def _trunk(x, layers):
    T = x.shape[1]
    tm = _tile(T, 512)
    for p in layers:
        x = _ffn(x, p["f1_pre"], p["f1_post"], p["f1_w_in"], p["f1_w_out"], tm)
        y_dn, y_sc, y_sr, y_at, y_g = _proj(x, p["m_pre"], p["wdn"], p["wsc"], p["wsr"], p["wat"], p["wg"], tm)
        out_a = _deltanet(y_dn, y_sc, y_sr, p["conv_w"], p["lc"], p["lr"], p["dn_norm_w"])
        out_b = _attention(y_at, p["sink"], tm)
        x = _merge(x, out_a, out_b, y_g, p["pa"], p["pb"], p["wo"], p["m_post"], tm)
        x = _ffn(x, p["f2_pre"], p["f2_post"], p["f2_w_in"], p["f2_w_out"], tm)
    return x


def kernel(x_prompt, x_sample, ffn1_norm_pre, ffn1_norm_post, ffn1_w_in, ffn1_w_out, mix_norm_pre, mix_norm_post, mix_w_in, dn_conv_w, dn_a_log, dn_dt_bias, dn_norm_w, attn_sink, w_branch_a, w_branch_b, mix_w_out, ffn2_norm_pre, ffn2_norm_post, ffn2_w_in, ffn2_w_out):
    depth = ffn1_w_in.shape[0]
    row = lambda v: v.reshape(1, -1).astype(F32)
    layers = []
    for l in range(depth):
        wdn, wsc, wsr, wat, wg = _split_w_in(mix_w_in[l])
        lc, lr = _gate_consts(dn_a_log[l], dn_dt_bias[l])
        layers.append(dict(
            f1_pre=row(ffn1_norm_pre[l]), f1_post=row(ffn1_norm_post[l]),
            f1_w_in=ffn1_w_in[l].astype(BF16), f1_w_out=ffn1_w_out[l].astype(BF16),
            m_pre=row(mix_norm_pre[l]), m_post=row(mix_norm_post[l]),
            wdn=wdn, wsc=wsc, wsr=wsr, wat=wat, wg=wg,
            conv_w=dn_conv_w[l].astype(F32), lc=lc, lr=lr, dn_norm_w=row(dn_norm_w[l]),
            sink=attn_sink[l].astype(F32),
            pa=w_branch_a[l].astype(BF16), pb=w_branch_b[l].astype(BF16), wo=mix_w_out[l].astype(BF16),
            f2_pre=row(ffn2_norm_pre[l]), f2_post=row(ffn2_norm_post[l]),
            f2_w_in=ffn2_w_in[l].astype(BF16), f2_w_out=ffn2_w_out[l].astype(BF16),
        ))
    return (_trunk(x_prompt, layers), _trunk(x_sample, layers))
```

```python
import functools

import jax
import jax.numpy as jnp
from jax import lax
from jax.experimental import pallas as pl
from jax.experimental.pallas import tpu as pltpu

F32 = jnp.float32
BF16 = jnp.bfloat16
HIGHEST = lax.Precision.HIGHEST

NORM_EPS = 1e-6
DN_HEADS = 4
DN_HEAD_DIM = 128
DN_WIDTH = DN_HEADS * DN_HEAD_DIM
CONV_WIDTH = 5
CHUNK = 64
ATTN_Q_HEADS = 8
ATTN_KV_HEADS = 2
ATTN_GROUP = ATTN_Q_HEADS // ATTN_KV_HEADS
ATTN_HEAD_DIM = 64
ATTN_WIDTH = ATTN_Q_HEADS * ATTN_HEAD_DIM
KV_WIDTH = ATTN_KV_HEADS * ATTN_HEAD_DIM
WINDOW = 128

V7X_LANES = 128
V7X_SUBLANES = 8
V7X_VMEM_LIMIT_BYTES = 56 * 1024 * 1024

SUPER = 2 * CHUNK
DN_GROUP = 4
assert SUPER == V7X_LANES and WINDOW == V7X_LANES and DN_HEAD_DIM == V7X_LANES
MASKED = -1e30

_EYE, _BONES, _TRI0, _SREL0, _M8_0, _E8_0, _E16_0, _E32_0 = 0, 1, 2, 4, 6, 8, 10, 12
_NMASK = 14
_PREPARE_STAGES_BEFORE_SLOT_WRITE = 11


def _rms(x, g):
    return x * lax.rsqrt(jnp.mean(x * x, axis=-1, keepdims=True) + NORM_EPS) * g


def _dot(a, b):
    return jnp.dot(a, b, preferred_element_type=F32)


def _dot_nt(a, b):
    return lax.dot_general(a, b, (((1,), (1,)), ((), ())), preferred_element_type=F32)


def _dot_tn(a, b):
    return lax.dot_general(a, b, (((0,), (0,)), ((), ())), preferred_element_type=F32)


def _softplus(x):
    return jnp.maximum(x, 0.0) + jnp.log1p(jnp.exp(-jnp.abs(x)))


def _resident(shape):
    nd = len(shape)
    return pl.BlockSpec(shape, lambda *_: (0,) * nd, pipeline_mode=pl.Buffered(1))


def _params(sem):
    return pltpu.CompilerParams(dimension_semantics=sem, vmem_limit_bytes=V7X_VMEM_LIMIT_BYTES)


def _ffn_body(x_ref, gpre_ref, gpost_ref, win_ref, wout_ref, o_ref, *, d_ff):
    x = x_ref[...]
    h = _rms(x, gpre_ref[...]).astype(BF16)
    gu = _dot(h, win_ref[...])
    gate = gu[:, :d_ff]
    up = gu[:, d_ff:]
    act = (gate * jax.nn.sigmoid(gate) * up).astype(BF16)
    y = _dot(act, wout_ref[...])
    o_ref[...] = x + 0.5 * _rms(y, gpost_ref[...])


def _ffn(x, gpre, gpost, w_in, w_out, tm):
    B, T, D = x.shape
    d_ff = w_out.shape[0]
    x2 = x.reshape(B * T, D)
    out = pl.pallas_call(
        functools.partial(_ffn_body, d_ff=d_ff),
        grid=(B * T // tm,),
        in_specs=[
            pl.BlockSpec((tm, D), lambda i: (i, 0)),
            _resident((1, D)),
            _resident((1, D)),
            _resident((D, 2 * d_ff)),
            _resident((d_ff, D)),
        ],
        out_specs=pl.BlockSpec((tm, D), lambda i: (i, 0)),
        out_shape=jax.ShapeDtypeStruct((B * T, D), F32),
        compiler_params=_params(("arbitrary",)),
        name="ffn",
    )(x2, gpre, gpost, w_in, w_out)
    return out.reshape(B, T, D)


def _proj_body(x_ref, g_ref, wdn_ref, wsc_ref, wsr_ref, wat_ref, wg_ref,
               odn_ref, osc_ref, osr_ref, oat_ref, og_ref):
    u = _rms(x_ref[...], g_ref[...]).astype(BF16)
    odn_ref[...] = _dot(u, wdn_ref[...]).astype(odn_ref.dtype)
    osc_ref[...] = _dot(u, wsc_ref[...])
    osr_ref[...] = _dot_nt(wsr_ref[...], u)
    oat_ref[...] = _dot(u, wat_ref[...]).astype(oat_ref.dtype)
    og_ref[...] = _dot(u, wg_ref[...]).astype(og_ref.dtype)


def _proj(x, g, wdn, wsc, wsr, wat, wg, tm):
    B, T, D = x.shape
    tok = lambda n: pl.BlockSpec((None, tm, n), lambda b, i: (b, i, 0))
    nsr = wsr.shape[0]
    return pl.pallas_call(
        _proj_body,
        grid=(B, T // tm),
        in_specs=[tok(D), _resident((1, D)), _resident(wdn.shape), _resident(wsc.shape),
                  _resident(wsr.shape), _resident(wat.shape), _resident(wg.shape)],
        out_specs=[tok(wdn.shape[1]), tok(wsc.shape[1]),
                   pl.BlockSpec((None, nsr, tm), lambda b, i: (b, 0, i)),
                   tok(wat.shape[1]), tok(wg.shape[1])],
        out_shape=[
            jax.ShapeDtypeStruct((B, T, wdn.shape[1]), BF16),
            jax.ShapeDtypeStruct((B, T, wsc.shape[1]), F32),
            jax.ShapeDtypeStruct((B, nsr, T), F32),
            jax.ShapeDtypeStruct((B, T, wat.shape[1]), BF16),
            jax.ShapeDtypeStruct((B, T, wg.shape[1]), BF16),
        ],
        compiler_params=_params(("arbitrary", "arbitrary")),
        name="proj",
    )(x, g, wdn, wsc, wsr, wat, wg)


def _build_masks(msk):
    ii = lax.broadcasted_iota(jnp.int32, (SUPER, SUPER), 0)
    jj = lax.broadcasted_iota(jnp.int32, (SUPER, SUPER), 1)
    same = lambda s: jnp.right_shift(ii, s) == jnp.right_shift(jj, s)
    f = lambda c: jnp.where(c, 1.0, 0.0).astype(F32)
    msk[_EYE] = f(ii == jj)
    msk[_BONES] = f(same(6))
    for d in range(2):
        after = (ii > jj) if d == 0 else (ii < jj)
        after_eq = (ii >= jj) if d == 0 else (ii <= jj)
        msk[_TRI0 + d] = f(same(6) & after_eq)
        msk[_SREL0 + d] = f(same(6) & after)
        msk[_M8_0 + d] = f(same(3) & after)
        msk[_E8_0 + d] = f(same(4) & jnp.logical_not(same(3)) & after)
        msk[_E16_0 + d] = f(same(5) & jnp.logical_not(same(4)) & after)
        msk[_E32_0 + d] = f(same(6) & jnp.logical_not(same(5)) & after)


def _prepare_stages(chains, qn, kn, vn, gcol, grow, msk, bufs):
    mm = lambda a, b: _dot(a.astype(BF16), b.astype(BF16))
    eye = msk[_EYE]
    st = []
    for d, r0, slot in chains:
        rows = pl.ds(r0, SUPER)
        c = dict(d=d, slot=slot, q=qn[rows, :], k=kn[rows, :])
        c["kb"] = c["k"].astype(BF16)
        c["vb"] = vn[rows, :].astype(BF16)
        gcv = gcol[rows, :]
        grv = grow[:, rows]
        c["gc_c"] = gcv[:, 2 + d:3 + d]
        c["beta_c"] = gcv[:, d:d + 1]
        c["gc_r"] = grv[2 + d:3 + d, :]
        c["beta_r"] = grv[d:d + 1, :]
        c["qg"] = _dot_nt(jnp.concatenate([c["q"].astype(BF16), c["kb"]], axis=0), c["kb"])
        st.append(c)
    yield
    for c in st:
        srel = msk[_SREL0 + c["d"]]
        dec = jnp.exp((c["gc_c"] - c["gc_r"]) * srel) * srel
        c["lm"] = c["qg"][SUPER:] * c["beta_c"] * dec
        c["intra"] = (c["qg"][:SUPER] * (dec + eye)).astype(BF16)
        c["x"] = c["lm"] * msk[_M8_0 + c["d"]]
        c["x2"] = mm(c["x"], c["x"])
    yield
    for c in st:
        c["x3"] = mm(c["x2"], c["x"])
        c["x4"] = mm(c["x2"], c["x2"])
    yield
    for c in st:
        p1 = eye - c["x"] + c["x2"] - c["x3"]
        c["inv"] = p1 + mm(p1, c["x4"])
    yield
    for lvl in (_E8_0, _E16_0, _E32_0):
        for c in st:
            c["t"] = mm(c["inv"], c["lm"] * msk[lvl + c["d"]])
        yield
        for c in st:
            c["inv"] = c["inv"] - mm(c["t"], c["inv"])
        yield
    for c in st:
        tb = c["inv"] * c["beta_r"]
        tbe = tb * jnp.exp(c["gc_r"])
        u = _dot(tb.astype(BF16), c["vb"])
        w = _dot(tbe.astype(BF16), c["kb"])
        c["wu"] = jnp.concatenate([w, u], axis=1).astype(BF16)
    yield
    for c in st:
        ap, nn, o0b, et = bufs[c["d"]]
        iw = _dot(c["intra"], c["wu"])
        c["qp"] = (c["q"] * jnp.exp(c["gc_c"]) - iw[:, :DN_HEAD_DIM]).astype(BF16)
        o0b[c["slot"]] = iw[:, DN_HEAD_DIM:]
    yield
    for c in st:
        ap, nn, o0b, et = bufs[c["d"]]
        for cc in range(2):
            sl = slice(cc * CHUNK, (cc + 1) * CHUNK)
            last = cc * CHUNK + (CHUNK - 1 if c["d"] == 0 else 0)
            tot = c["gc_c"][last:last + 1, :]
            kdec = (c["k"][sl] * jnp.exp(tot - c["gc_c"][sl])).astype(BF16)
            mn = _dot_tn(kdec, c["wu"][sl])
            ap[c["slot"], cc] = jnp.concatenate([mn[:, :DN_HEAD_DIM].astype(BF16), c["qp"][sl]], axis=0)
            nn[c["slot"], cc] = mn[:, DN_HEAD_DIM:]
            et[c["slot"], cc] = jnp.broadcast_to(jnp.exp(tot), (V7X_SUBLANES, DN_HEAD_DIM))
    yield


def _scan_stages(rows_f, rows_b, bufs, s_f, s_b, oacc):
    state = [s_f[...], s_b[...]]
    for slot, r0s in enumerate(zip(rows_f, rows_b)):
        for step in range(2):
            for d in range(2):
                ap, nn, o0b, et = bufs[d]
                cc = step if d == 0 else 1 - step
                s = state[d]
                r = _dot(ap[slot, cc], s.astype(BF16))
                rows = pl.ds(r0s[d] + cc * CHUNK, CHUNK)
                oacc[rows, :] += r[DN_HEAD_DIM:] + o0b[slot, cc * CHUNK:(cc + 1) * CHUNK, :]
                s3 = s.reshape(DN_HEAD_DIM // V7X_SUBLANES, V7X_SUBLANES, DN_HEAD_DIM) * et[slot, cc][None]
                state[d] = s3.reshape(DN_HEAD_DIM, DN_HEAD_DIM) - r[:DN_HEAD_DIM] + nn[slot, cc]
            yield
    s_f[...] = state[0]
    s_b[...] = state[1]
    yield


def _interleave(*stage_generators):
    live = list(stage_generators)
    while live:
        for gen in list(live):
            try:
                next(gen)
            except StopIteration:
                live.remove(gen)


def _dn_body(q_ref, k_ref, v_ref, z_ref, gc_ref, gr_ref, cwq_ref, cwk_ref, cwv_ref,
             lc_ref, lr_ref, nw_ref, o_ref,
             pad, qn, kn, vn, gcol, grow, oacc, s_f, s_b, msk,
             ap_f, nn_f, o0_f, et_f, ap_b, nn_b, o0_b, et_b, *, T, rt, group, cum_unroll):
    nt = T // rt
    ns = T // SUPER
    halo = V7X_SUBLANES
    _build_masks(msk)

    lc = lc_ref[...]
    lane = lax.broadcasted_iota(jnp.int32, (rt, V7X_LANES), 1)

    def gate_tile(i, _):
        r = pl.ds(pl.multiple_of(i * rt, rt), rt)
        x = gc_ref[r, :]
        g = -jnp.exp(lc[0:1, :]) * _softplus(x + lc[1:2, :])
        gcol[r, :] = jnp.where(lane < 2, jax.nn.sigmoid(x), g)
        oacc[r, :] = jnp.zeros((rt, DN_HEAD_DIM), F32)
        return 0

    lax.fori_loop(0, nt, gate_tile, 0)
    xr = gr_ref[...]
    lr = lr_ref[...]
    rowi = lax.broadcasted_iota(jnp.int32, xr.shape, 0)
    grow[...] = jnp.where(rowi < 2, jax.nn.sigmoid(xr),
                          -jnp.exp(lr[:, 0:1]) * _softplus(xr + lr[:, 1:2]))

    tri_fb = jnp.concatenate([msk[_TRI0], msk[_TRI0 + 1]], axis=0).astype(BF16)
    tri_bf = jnp.concatenate([msk[_TRI0 + 1], msk[_TRI0]], axis=1).astype(BF16)
    lane_s = lax.broadcasted_iota(jnp.int32, (SUPER, V7X_LANES), 1)
    row_s = lax.broadcasted_iota(jnp.int32, (V7X_SUBLANES, SUPER), 0)

    def split3(x, axis):
        hi = x.astype(BF16).astype(F32)
        r1 = x - hi
        mid = r1.astype(BF16).astype(F32)
        lo = r1 - mid
        parts = [hi, mid, lo] + ([jnp.zeros_like(x)] if axis == 0 else [])
        return jnp.concatenate(parts, axis=axis).astype(BF16)

    def cum_tile(i, _):
        tiles = []
        for uu in range(cum_unroll):
            rows = pl.ds(pl.multiple_of((i * cum_unroll + uu) * SUPER, SUPER), SUPER)
            gcv = gcol[rows, :]
            grv = grow[:, rows]
            cs = _dot(tri_fb, split3(gcv, 1))
            rs = _dot(split3(grv, 0), tri_bf)
            tiles.append((rows, gcv, grv, cs, rs))
        for rows, gcv, grv, cs, rs in tiles:
            cs = cs[:, :V7X_LANES] + cs[:, V7X_LANES:2 * V7X_LANES] + cs[:, 2 * V7X_LANES:]
            rs = rs[:V7X_SUBLANES] + rs[V7X_SUBLANES:2 * V7X_SUBLANES] + rs[2 * V7X_SUBLANES:3 * V7X_SUBLANES]
            gcol[rows, :] = jnp.where(lane_s == 2, cs[:SUPER], jnp.where(lane_s == 3, cs[SUPER:], gcv))
            grow[:, rows] = jnp.where(row_s == 2, rs[:, :SUPER], jnp.where(row_s == 3, rs[:, SUPER:], grv))
        return 0

    lax.fori_loop(0, ns // cum_unroll, cum_tile, 0)

    pad[0:halo, :] = jnp.zeros((halo, DN_HEAD_DIM), F32)
    pad[T + halo:T + 2 * halo, :] = jnp.zeros((halo, DN_HEAD_DIM), F32)
    for src, cw_ref, dst, scale in ((q_ref, cwq_ref, qn, DN_HEAD_DIM ** -0.5),
                                    (k_ref, cwk_ref, kn, 1.0), (v_ref, cwv_ref, vn, None)):
        def fill(i, _, src=src):
            pad[pl.ds(pl.multiple_of(i * rt, rt) + halo, rt), :] = \
                src[pl.ds(pl.multiple_of(i * rt, rt), rt), :].astype(F32)
            return 0

        lax.fori_loop(0, nt, fill, 0)
        cw = cw_ref[...]

        def conv_tile(i, _, dst=dst, scale=scale, cw=cw):
            base = i * rt + halo - CONV_WIDTH // 2
            acc = pad[pl.ds(base, rt), :] * cw[0:1, :]
            for j in range(1, CONV_WIDTH):
                acc = acc + pad[pl.ds(base + j, rt), :] * cw[j:j + 1, :]
            y = acc * jax.nn.sigmoid(acc)
            if scale is not None:
                y = y * (lax.rsqrt(jnp.sum(y * y, axis=-1, keepdims=True) + NORM_EPS) * scale)
            dst[pl.ds(pl.multiple_of(i * rt, rt), rt), :] = y
            return 0

        lax.fori_loop(0, nt, conv_tile, 0)

    s_f[...] = jnp.zeros((DN_HEAD_DIM, DN_HEAD_DIM), F32)
    s_b[...] = jnp.zeros((DN_HEAD_DIM, DN_HEAD_DIM), F32)
    fwd_row = lambda g, uu: pl.multiple_of((g * group + uu) * SUPER, SUPER)
    bwd_row = lambda g, uu: pl.multiple_of((ns - 1 - (g * group + uu)) * SUPER, SUPER)

    bufs = ((ap_f, nn_f, o0_f, et_f), (ap_b, nn_b, o0_b, et_b))

    def prepare(g):
        chains = [(d, (fwd_row, bwd_row)[d](g, uu), uu) for uu in range(group) for d in range(2)]
        return _prepare_stages(chains, qn, kn, vn, gcol, grow, msk, bufs)

    def scan(g):
        return _scan_stages([fwd_row(g, uu) for uu in range(group)], [bwd_row(g, uu) for uu in range(group)],
                            bufs, s_f, s_b, oacc)

    _interleave(prepare(0))

    def group_step(g, _):
        _interleave(prepare(g + 1), scan(g))
        return 0

    n_groups = ns // group
    lax.fori_loop(0, n_groups - 1, group_step, 0)
    _interleave(scan(n_groups - 1))

    nw = nw_ref[...]

    def out_tile(i, _):
        r = pl.ds(pl.multiple_of(i * rt, rt), rt)
        z = z_ref[r, :].astype(F32)
        o_ref[r, :] = (_rms(oacc[r, :], nw) * (z * jax.nn.sigmoid(z))).astype(o_ref.dtype)
        return 0

    lax.fori_loop(0, nt, out_tile, 0)


def _largest_divisor(n, cap):
    return max(d for d in range(1, cap + 1) if n % d == 0)


def _deltanet(y_dn, y_sc, y_sr, conv_w, lc, lr, nw):
    B, T, _ = y_dn.shape
    H, hd = DN_HEADS, DN_HEAD_DIM
    rt = min(256, T)
    ns = T // SUPER
    group = _largest_divisor(ns, DN_GROUP)
    assert 2 * group + 1 <= _PREPARE_STAGES_BEFORE_SLOT_WRITE
    col = lambda off: pl.BlockSpec((None, T, hd), lambda b, h: (b, 0, off + h))
    cw = lambda off: pl.BlockSpec((CONV_WIDTH, hd), lambda b, h: (0, off + h))
    tok = pltpu.VMEM((T, hd), F32)
    slots = [pltpu.VMEM((group, 2, hd + CHUNK, hd), BF16), pltpu.VMEM((group, 2, hd, hd), F32),
             pltpu.VMEM((group, SUPER, hd), F32), pltpu.VMEM((group, 2, V7X_SUBLANES, hd), F32)]
    return pl.pallas_call(
        functools.partial(_dn_body, T=T, rt=rt, group=group, cum_unroll=_largest_divisor(ns, 4)),
        grid=(B, H),
        in_specs=[col(0), col(H), col(2 * H), col(3 * H),
                  pl.BlockSpec((None, T, V7X_LANES), lambda b, h: (b, 0, h)),
                  pl.BlockSpec((None, V7X_SUBLANES, T), lambda b, h: (b, h, 0)),
                  cw(0), cw(H), cw(2 * H),
                  pl.BlockSpec((None, 2, V7X_LANES), lambda b, h: (h, 0, 0)),
                  pl.BlockSpec((None, V7X_SUBLANES, 2), lambda b, h: (h, 0, 0)),
                  pl.BlockSpec((1, hd), lambda b, h: (0, 0))],
        out_specs=pl.BlockSpec((None, T, hd), lambda b, h: (b, 0, h)),
        out_shape=jax.ShapeDtypeStruct((B, T, DN_WIDTH), BF16),
        scratch_shapes=[pltpu.VMEM((T + 2 * V7X_SUBLANES, hd), F32), tok, tok, tok,
                        pltpu.VMEM((T, V7X_LANES), F32), pltpu.VMEM((V7X_SUBLANES, T), F32), tok,
                        pltpu.VMEM((hd, hd), F32), pltpu.VMEM((hd, hd), F32),
                        pltpu.VMEM((_NMASK, SUPER, SUPER), F32)] + slots + slots,
        compiler_params=_params(("arbitrary", "arbitrary")),
        name="deltanet",
    )(y_dn, y_dn, y_dn, y_dn, y_sc, y_sr, conv_w, conv_w, conv_w, lc, lr, nw)


def _attn_body(sink_ref, q_ref, k_ref, v_ref, o_ref, k2, v2, bias, *, T, tq, rt):
    i = pl.program_id(1)
    nb = T // WINDOW
    half = ATTN_HEAD_DIM
    lane = lax.broadcasted_iota(jnp.int32, (WINDOW, V7X_LANES), 1)
    lo_mask = jnp.where(lane < half, 1.0, 0.0).astype(F32)
    hi_mask = 1.0 - lo_mask

    @pl.when(i == 0)
    def _init():
        zeros = jnp.zeros((WINDOW, V7X_LANES), BF16)
        for kv in range(ATTN_KV_HEADS):
            for r0 in (0, T + WINDOW):
                k2[kv, r0:r0 + WINDOW, :] = zeros
                v2[kv, r0:r0 + WINDOW, 0:V7X_LANES] = zeros
                v2[kv, r0:r0 + WINDOW, V7X_LANES:2 * V7X_LANES] = jnp.ones((WINDOW, V7X_LANES), BF16)
        lane_r = lax.broadcasted_iota(jnp.int32, (rt, V7X_LANES), 1)

        def fill(t, _):
            src = pl.ds(pl.multiple_of(t * rt, rt), rt)
            dst = pl.ds(pl.multiple_of(t * rt, rt) + WINDOW, rt)
            kf = k_ref[src, :].astype(F32)
            vf = v_ref[src, :].astype(F32)
            for kv in range(ATTN_KV_HEADS):
                sel = (lane_r < half) if kv == 0 else (lane_r >= half)
                km = jnp.where(sel, kf, 0.0)
                vm = jnp.where(sel, vf, 0.0)
                k2[kv, dst, :] = (km + pltpu.roll(km, half, 1)).astype(BF16)
                v2[kv, dst, 0:V7X_LANES] = (vm + pltpu.roll(vm, half, 1)).astype(BF16)
                v2[kv, dst, V7X_LANES:2 * V7X_LANES] = jnp.ones((rt, V7X_LANES), BF16)
            return 0

        lax.fori_loop(0, T // rt, fill, 0)
        a = lax.broadcasted_iota(jnp.int32, (WINDOW, 3 * WINDOW), 0)
        j = lax.broadcasted_iota(jnp.int32, (WINDOW, 3 * WINDOW), 1)
        dist = jnp.abs(j - WINDOW - a)
        for h in range(ATTN_Q_HEADS):
            slope = 2.0 ** (-8.0 * (h + 1) / ATTN_Q_HEADS)
            bias[h] = jnp.where(dist <= WINDOW, -slope * dist.astype(F32), MASKED)

    jrow = lax.broadcasted_iota(jnp.int32, (1, 3 * WINDOW), 1)

    def block(qi, _):
        gi = i * (tq // WINDOW) + qi
        qr = pl.ds(pl.multiple_of(qi * WINDOW, WINDOW), WINDOW)
        win = pl.ds(pl.multiple_of(gi * WINDOW, WINDOW), 3 * WINDOW)
        first_valid = jnp.where(gi == 0, WINDOW, 0)
        end_valid = jnp.where(gi == nb - 1, 2 * WINDOW, 3 * WINDOW)
        outside = (jrow < first_valid) | (jrow >= end_valid)
        pos_bias = jnp.where(outside, MASKED, 0.0)
        for kv in range(ATTN_KV_HEADS):
            kw = k2[kv, win, :]
            vw = v2[kv, win, :]
            for pair in range(ATTN_GROUP // 2):
                c0 = (kv * (ATTN_GROUP // 2) + pair) * V7X_LANES
                qf = q_ref[qr, c0:c0 + V7X_LANES].astype(F32) * (ATTN_HEAD_DIM ** -0.5)
                tiles = []
                for hh, hmask in enumerate((lo_mask, hi_mask)):
                    h = kv * ATTN_GROUP + pair * 2 + hh
                    sink = sink_ref[h]
                    s = _dot_nt((qf * hmask).astype(BF16), kw) + bias[h] + pos_bias
                    m = jnp.maximum(jnp.max(s, axis=-1, keepdims=True), sink)
                    p = jnp.exp(s - m).astype(BF16)
                    ov = _dot(p, vw)
                    den = ov[:, V7X_LANES:] + jnp.exp(sink - m)
                    tiles.append(ov[:, :V7X_LANES] / den)
                o_ref[qr, c0:c0 + V7X_LANES] = jnp.where(lane < half, tiles[0], tiles[1]).astype(o_ref.dtype)
        return 0

    lax.fori_loop(0, tq // WINDOW, block, 0)


def _attention(y_at, sink, tq):
    B, T, _ = y_at.shape
    rt = min(256, T)
    return pl.pallas_call(
        functools.partial(_attn_body, T=T, tq=tq, rt=rt),
        grid=(B, T // tq),
        in_specs=[pl.BlockSpec(memory_space=pltpu.SMEM),
                  pl.BlockSpec((None, tq, ATTN_WIDTH), lambda b, i: (b, i, 0)),
                  pl.BlockSpec((None, T, KV_WIDTH), lambda b, i: (b, 0, ATTN_WIDTH // KV_WIDTH)),
                  pl.BlockSpec((None, T, KV_WIDTH), lambda b, i: (b, 0, ATTN_WIDTH // KV_WIDTH + 1))],
        out_specs=pl.BlockSpec((None, tq, ATTN_WIDTH), lambda b, i: (b, i, 0)),
        out_shape=jax.ShapeDtypeStruct((B, T, ATTN_WIDTH), BF16),
        scratch_shapes=[pltpu.VMEM((ATTN_KV_HEADS, T + 2 * WINDOW, V7X_LANES), BF16),
                        pltpu.VMEM((ATTN_KV_HEADS, T + 2 * WINDOW, 2 * V7X_LANES), BF16),
                        pltpu.VMEM((ATTN_Q_HEADS, WINDOW, 3 * WINDOW), F32)],
        compiler_params=_params(("arbitrary", "arbitrary")),
        name="window_attn",
    )(sink, y_at, y_at, y_at)


def _merge_body(x_ref, a_ref, b_ref, g_ref, pa_ref, pb_ref, wo_ref, gpost_ref, o_ref, *, d_model):
    ya = _dot(a_ref[...], pa_ref[...])
    yb = _dot(b_ref[...], pb_ref[...])
    g = g_ref[...].astype(F32)
    merged = jax.nn.sigmoid(g[:, :d_model]) * ya + jax.nn.sigmoid(g[:, d_model:]) * yb
    mix = _dot(merged.astype(BF16), wo_ref[...])
    o_ref[...] = x_ref[...] + _rms(mix, gpost_ref[...])


def _merge(x, out_a, out_b, y_g, pa, pb, wo, gpost, tm):
    B, T, D = x.shape
    tok = lambda n: pl.BlockSpec((None, tm, n), lambda b, i: (b, i, 0))
    return pl.pallas_call(
        functools.partial(_merge_body, d_model=D),
        grid=(B, T // tm),
        in_specs=[tok(D), tok(out_a.shape[-1]), tok(out_b.shape[-1]), tok(y_g.shape[-1]),
                  _resident(pa.shape), _resident(pb.shape), _resident(wo.shape), _resident((1, D))],
        out_specs=tok(D),
        out_shape=jax.ShapeDtypeStruct((B, T, D), F32),
        compiler_params=_params(("arbitrary", "arbitrary")),
        name="merge",
    )(x, out_a, out_b, y_g, pa, pb, wo, gpost)


def _split_w_in(w_in):
    D = w_in.shape[0]
    H = DN_HEADS
    o_small = 4 * DN_WIDTH
    o_attn = o_small + 4 * H
    o_gate = o_attn + ATTN_WIDTH + 2 * KV_WIDTH
    wdn = w_in[:, :o_small].astype(BF16)
    small = w_in[:, o_small:o_attn].reshape(D, 4, H)
    small = jnp.transpose(small, (0, 2, 1))
    wsc = jnp.pad(small, ((0, 0), (0, 0), (0, V7X_LANES - 4))).reshape(D, H * V7X_LANES).astype(BF16)
    wsr = jnp.pad(small, ((0, 0), (0, 0), (0, V7X_SUBLANES - 4))).reshape(D, H * V7X_SUBLANES).T.astype(BF16)
    wat = w_in[:, o_attn:o_gate].astype(BF16)
    wg = w_in[:, o_gate:].astype(BF16)
    return wdn, wsc, wsr, wat, wg


def _gate_consts(a_log, dt_bias):
    H = DN_HEADS
    both = jnp.stack([a_log, dt_bias], axis=0).astype(F32)
    per_head = jnp.transpose(both, (2, 0, 1))
    lc = jnp.zeros((H, 2, V7X_LANES), F32).at[:, :, 2:4].set(per_head)
    lr = jnp.zeros((H, V7X_SUBLANES, 2), F32).at[:, 2:4, :].set(jnp.transpose(per_head, (0, 2, 1)))
    return lc, lr


def _tile(n, pref):
    t = min(pref, n)
    assert n % t == 0
    return t


def _trunk(x, layers):
    T = x.shape[1]
    tm = _tile(T, 512)
    for p in layers:
        x = _ffn(x, p["f1_pre"], p["f1_post"], p["f1_w_in"], p["f1_w_out"], tm)
        y_dn, y_sc, y_sr, y_at, y_g = _proj(x, p["m_pre"], p["wdn"], p["wsc"], p["wsr"], p["wat"], p["wg"], tm)
        out_a = _deltanet(y_dn, y_sc, y_sr, p["conv_w"], p["lc"], p["lr"], p["dn_norm_w"])
        out_b = _attention(y_at, p["sink"], tm)
        x = _merge(x, out_a, out_b, y_g, p["pa"], p["pb"], p["wo"], p["m_post"], tm)
        x = _ffn(x, p["f2_pre"], p["f2_post"], p["f2_w_in"], p["f2_w_out"], tm)
    return x


def kernel(x_prompt, x_sample, ffn1_norm_pre, ffn1_norm_post, ffn1_w_in, ffn1_w_out, mix_norm_pre, mix_norm_post, mix_w_in, dn_conv_w, dn_a_log, dn_dt_bias, dn_norm_w, attn_sink, w_branch_a, w_branch_b, mix_w_out, ffn2_norm_pre, ffn2_norm_post, ffn2_w_in, ffn2_w_out):
    depth = ffn1_w_in.shape[0]
    row = lambda v: v.reshape(1, -1).astype(F32)
    layers = []
    for l in range(depth):
        wdn, wsc, wsr, wat, wg = _split_w_in(mix_w_in[l])
        lc, lr = _gate_consts(dn_a_log[l], dn_dt_bias[l])
        layers.append(dict(
            f1_pre=row(ffn1_norm_pre[l]), f1_post=row(ffn1_norm_post[l]),
            f1_w_in=ffn1_w_in[l].astype(BF16), f1_w_out=ffn1_w_out[l].astype(BF16),
            m_pre=row(mix_norm_pre[l]), m_post=row(mix_norm_post[l]),
            wdn=wdn, wsc=wsc, wsr=wsr, wat=wat, wg=wg,
            conv_w=dn_conv_w[l].astype(F32), lc=lc, lr=lr, dn_norm_w=row(dn_norm_w[l]),
            sink=attn_sink[l].astype(F32),
            pa=w_branch_a[l].astype(BF16), pb=w_branch_b[l].astype(BF16), wo=mix_w_out[l].astype(BF16),
            f2_pre=row(ffn2_norm_pre[l]), f2_post=row(ffn2_norm_post[l]),
            f2_w_in=ffn2_w_in[l].astype(BF16), f2_w_out=ffn2_w_out[l].astype(BF16),
        ))
    return (_trunk(x_prompt, layers), _trunk(x_sample, layers))
```

```python
import functools

import jax
import jax.numpy as jnp
from jax import lax
from jax.experimental import pallas as pl
from jax.experimental.pallas import tpu as pltpu

F32 = jnp.float32
BF16 = jnp.bfloat16
HIGHEST = lax.Precision.HIGHEST

NORM_EPS = 1e-6
DN_HEADS = 4
DN_HEAD_DIM = 128
DN_WIDTH = DN_HEADS * DN_HEAD_DIM
CONV_WIDTH = 5
CHUNK = 64
ATTN_Q_HEADS = 8
ATTN_KV_HEADS = 2
ATTN_GROUP = ATTN_Q_HEADS // ATTN_KV_HEADS
ATTN_HEAD_DIM = 64
ATTN_WIDTH = ATTN_Q_HEADS * ATTN_HEAD_DIM
KV_WIDTH = ATTN_KV_HEADS * ATTN_HEAD_DIM
WINDOW = 128

V7X_LANES = 128
V7X_SUBLANES = 8
V7X_VMEM_LIMIT_BYTES = 56 * 1024 * 1024

SUPER = 2 * CHUNK
DN_GROUP = 4
assert SUPER == V7X_LANES and WINDOW == V7X_LANES and DN_HEAD_DIM == V7X_LANES
MASKED = -1e30

_EYE, _BONES, _TRI0, _SREL0, _M8_0, _E8_0, _E16_0, _E32_0 = 0, 1, 2, 4, 6, 8, 10, 12
_NMASK = 14
_PREPARE_STAGES_BEFORE_SLOT_WRITE = 11


def _rms(x, g):
    return x * lax.rsqrt(jnp.mean(x * x, axis=-1, keepdims=True) + NORM_EPS) * g


def _dot(a, b):
    return jnp.dot(a, b, preferred_element_type=F32)


def _dot_nt(a, b):
    return lax.dot_general(a, b, (((1,), (1,)), ((), ())), preferred_element_type=F32)


def _dot_tn(a, b):
    return lax.dot_general(a, b, (((0,), (0,)), ((), ())), preferred_element_type=F32)


def _softplus(x):
    return jnp.maximum(x, 0.0) + jnp.log1p(jnp.exp(-jnp.abs(x)))


def _resident(shape):
    nd = len(shape)
    return pl.BlockSpec(shape, lambda *_: (0,) * nd, pipeline_mode=pl.Buffered(1))


def _params(sem):
    return pltpu.CompilerParams(dimension_semantics=sem, vmem_limit_bytes=V7X_VMEM_LIMIT_BYTES)


def _ffn_body(x_ref, gpre_ref, gpost_ref, win_ref, wout_ref, o_ref, *, d_ff):
    x = x_ref[...]
    h = _rms(x, gpre_ref[...]).astype(BF16)
    gu = _dot(h, win_ref[...])
    gate = gu[:, :d_ff]
    up = gu[:, d_ff:]
    act = (gate * jax.nn.sigmoid(gate) * up).astype(BF16)
    y = _dot(act, wout_ref[...])
    o_ref[...] = x + 0.5 * _rms(y, gpost_ref[...])


def _ffn(x, gpre, gpost, w_in, w_out, tm):
    B, T, D = x.shape
    d_ff = w_out.shape[0]
    x2 = x.reshape(B * T, D)
    out = pl.pallas_call(
        functools.partial(_ffn_body, d_ff=d_ff),
        grid=(B * T // tm,),
        in_specs=[
            pl.BlockSpec((tm, D), lambda i: (i, 0)),
            _resident((1, D)),
            _resident((1, D)),
            _resident((D, 2 * d_ff)),
            _resident((d_ff, D)),
        ],
        out_specs=pl.BlockSpec((tm, D), lambda i: (i, 0)),
        out_shape=jax.ShapeDtypeStruct((B * T, D), F32),
        compiler_params=_params(("arbitrary",)),
        name="ffn",
    )(x2, gpre, gpost, w_in, w_out)
    return out.reshape(B, T, D)


def _proj_body(x_ref, g_ref, wdn_ref, wsr_ref, wat_ref, wg_ref, odn_ref, osr_ref, oat_ref, og_ref):
    u = _rms(x_ref[...], g_ref[...]).astype(BF16)
    odn_ref[...] = _dot(u, wdn_ref[...]).astype(odn_ref.dtype)
    osr_ref[...] = _dot_nt(wsr_ref[...], u)
    oat_ref[...] = _dot(u, wat_ref[...]).astype(oat_ref.dtype)
    og_ref[...] = _dot(u, wg_ref[...]).astype(og_ref.dtype)


def _proj(x, g, wdn, wsr, wat, wg, tm):
    B, T, D = x.shape
    tok = lambda n: pl.BlockSpec((None, tm, n), lambda b, i: (b, i, 0))
    nsr = wsr.shape[0]
    return pl.pallas_call(
        _proj_body,
        grid=(B, T // tm),
        in_specs=[tok(D), _resident((1, D)), _resident(wdn.shape),
                  _resident(wsr.shape), _resident(wat.shape), _resident(wg.shape)],
        out_specs=[tok(wdn.shape[1]),
                   pl.BlockSpec((None, nsr, tm), lambda b, i: (b, 0, i)),
                   tok(wat.shape[1]), tok(wg.shape[1])],
        out_shape=[
            jax.ShapeDtypeStruct((B, T, wdn.shape[1]), BF16),
            jax.ShapeDtypeStruct((B, nsr, T), F32),
            jax.ShapeDtypeStruct((B, T, wat.shape[1]), BF16),
            jax.ShapeDtypeStruct((B, T, wg.shape[1]), BF16),
        ],
        compiler_params=_params(("arbitrary", "arbitrary")),
        name="proj",
    )(x, g, wdn, wsr, wat, wg)


def _build_masks(msk):
    ii = lax.broadcasted_iota(jnp.int32, (SUPER, SUPER), 0)
    jj = lax.broadcasted_iota(jnp.int32, (SUPER, SUPER), 1)
    same = lambda s: jnp.right_shift(ii, s) == jnp.right_shift(jj, s)
    f = lambda c: jnp.where(c, 1.0, 0.0).astype(F32)
    msk[_EYE] = f(ii == jj)
    msk[_BONES] = f(same(6))
    for d in range(2):
        after = (ii > jj) if d == 0 else (ii < jj)
        after_eq = (ii >= jj) if d == 0 else (ii <= jj)
        msk[_TRI0 + d] = f(same(6) & after_eq)
        msk[_SREL0 + d] = f(same(6) & after)
        msk[_M8_0 + d] = f(same(3) & after)
        msk[_E8_0 + d] = f(same(4) & jnp.logical_not(same(3)) & after)
        msk[_E16_0 + d] = f(same(5) & jnp.logical_not(same(4)) & after)
        msk[_E32_0 + d] = f(same(6) & jnp.logical_not(same(5)) & after)


def _prepare_stages(chains, qn, kn, vn, gcol, grow, msk, bufs):
    mm = lambda a, b: _dot(a.astype(BF16), b.astype(BF16))
    eye = msk[_EYE]
    st = []
    for d, r0, slot in chains:
        rows = pl.ds(r0, SUPER)
        c = dict(d=d, slot=slot, q=qn[rows, :], k=kn[rows, :])
        c["kb"] = c["k"].astype(BF16)
        c["vb"] = vn[rows, :].astype(BF16)
        gcv = gcol[rows, :]
        grv = grow[:, rows]
        c["gc_c"] = gcv[:, 2 + d:3 + d]
        c["beta_c"] = gcv[:, d:d + 1]
        c["gc_r"] = grv[2 + d:3 + d, :]
        c["beta_r"] = grv[d:d + 1, :]
        c["qg"] = _dot_nt(jnp.concatenate([c["q"].astype(BF16), c["kb"]], axis=0), c["kb"])
        st.append(c)
    yield
    for c in st:
        srel = msk[_SREL0 + c["d"]]
        dec = jnp.exp((c["gc_c"] - c["gc_r"]) * srel) * srel
        c["lm"] = c["qg"][SUPER:] * c["beta_c"] * dec
        c["intra"] = (c["qg"][:SUPER] * (dec + eye)).astype(BF16)
        c["x"] = c["lm"] * msk[_M8_0 + c["d"]]
        c["x2"] = mm(c["x"], c["x"])
    yield
    for c in st:
        x34 = mm(c["x2"], jnp.concatenate([c["x"], c["x2"]], axis=1))
        c["x3"] = x34[:, :SUPER]
        c["x4"] = x34[:, SUPER:]
    yield
    for c in st:
        p1 = eye - c["x"] + c["x2"] - c["x3"]
        c["inv"] = p1 + mm(p1, c["x4"])
    yield
    for lvl in (_E8_0, _E16_0, _E32_0):
        for c in st:
            c["t"] = mm(c["inv"], c["lm"] * msk[lvl + c["d"]])
        yield
        for c in st:
            c["inv"] = c["inv"] - mm(c["t"], c["inv"])
        yield
    for c in st:
        tb = c["inv"] * c["beta_r"]
        tbe = tb * jnp.exp(c["gc_r"])
        u = _dot(tb.astype(BF16), c["vb"])
        w = _dot(tbe.astype(BF16), c["kb"])
        c["wu"] = jnp.concatenate([w, u], axis=1).astype(BF16)
    yield
    for c in st:
        ap, nn, o0b, et = bufs[c["d"]]
        iw = _dot(c["intra"], c["wu"])
        c["qp"] = (c["q"] * jnp.exp(c["gc_c"]) - iw[:, :DN_HEAD_DIM]).astype(BF16)
        o0b[c["slot"]] = iw[:, DN_HEAD_DIM:]
    yield
    for c in st:
        ap, nn, o0b, et = bufs[c["d"]]
        for cc in range(2):
            sl = slice(cc * CHUNK, (cc + 1) * CHUNK)
            last = cc * CHUNK + (CHUNK - 1 if c["d"] == 0 else 0)
            tot = c["gc_c"][last:last + 1, :]
            kdec = (c["k"][sl] * jnp.exp(tot - c["gc_c"][sl])).astype(BF16)
            mn = _dot_tn(kdec, c["wu"][sl])
            ap[c["slot"], cc] = jnp.concatenate([mn[:, :DN_HEAD_DIM].astype(BF16), c["qp"][sl]], axis=0)
            nn[c["slot"], cc] = mn[:, DN_HEAD_DIM:]
            et[c["slot"], cc] = jnp.broadcast_to(jnp.exp(tot), (V7X_SUBLANES, DN_HEAD_DIM))
    yield


def _scan_stages(rows_f, rows_b, bufs, s_f, s_b, oacc):
    state = [s_f[...], s_b[...]]
    for slot, r0s in enumerate(zip(rows_f, rows_b)):
        for step in range(2):
            for d in range(2):
                ap, nn, o0b, et = bufs[d]
                cc = step if d == 0 else 1 - step
                s = state[d]
                r = _dot(ap[slot, cc], s.astype(BF16))
                rows = pl.ds(r0s[d] + cc * CHUNK, CHUNK)
                oacc[rows, :] += r[DN_HEAD_DIM:] + o0b[slot, cc * CHUNK:(cc + 1) * CHUNK, :]
                s3 = s.reshape(DN_HEAD_DIM // V7X_SUBLANES, V7X_SUBLANES, DN_HEAD_DIM) * et[slot, cc][None]
                state[d] = s3.reshape(DN_HEAD_DIM, DN_HEAD_DIM) - r[:DN_HEAD_DIM] + nn[slot, cc]
            yield
    s_f[...] = state[0]
    s_b[...] = state[1]
    yield


def _interleave(*stage_generators):
    live = list(stage_generators)
    while live:
        for gen in list(live):
            try:
                next(gen)
            except StopIteration:
                live.remove(gen)


def _dn_body(q_ref, k_ref, v_ref, z_ref, gr_ref, cwq_ref, cwk_ref, cwv_ref, lr_ref, nw_ref, o_ref,
             pad, qn, kn, vn, gcol, grow, oacc, s_f, s_b, msk,
             ap_f, nn_f, o0_f, et_f, ap_b, nn_b, o0_b, et_b, *, T, rt, group, cum_unroll):
    nt = T // rt
    ns = T // SUPER
    halo = V7X_SUBLANES
    nsub = V7X_SUBLANES
    _build_masks(msk)

    xr = gr_ref[...]
    lr = lr_ref[...]
    rowi = lax.broadcasted_iota(jnp.int32, xr.shape, 0)
    grow[...] = jnp.where(rowi < 2, jax.nn.sigmoid(xr),
                          -jnp.exp(lr[:, 0:1]) * _softplus(xr + lr[:, 1:2]))

    tri_bf = jnp.concatenate([msk[_TRI0 + 1], msk[_TRI0]], axis=1).astype(BF16)
    row_s = lax.broadcasted_iota(jnp.int32, (nsub, SUPER), 0)
    sel_r = lax.broadcasted_iota(jnp.int32, (4 * nsub, V7X_LANES), 0)
    sel_n = lax.broadcasted_iota(jnp.int32, (4 * nsub, V7X_LANES), 1)
    sel = jnp.where(((sel_r & (nsub - 1)) == sel_n) & (sel_r < 3 * nsub), 1.0, 0.0).astype(BF16)

    def split3(x):
        hi = x.astype(BF16).astype(F32)
        r1 = x - hi
        mid = r1.astype(BF16).astype(F32)
        lo = r1 - mid
        return jnp.concatenate([hi, mid, lo, jnp.zeros_like(x)], axis=0).astype(BF16)

    def cum_tile(i, _):
        tiles = []
        for uu in range(cum_unroll):
            rows = pl.ds(pl.multiple_of((i * cum_unroll + uu) * SUPER, SUPER), SUPER)
            grv = grow[:, rows]
            tiles.append((rows, grv, _dot(split3(grv), tri_bf)))
        cols = []
        for rows, grv, rs in tiles:
            rs = rs[:nsub] + rs[nsub:2 * nsub] + rs[2 * nsub:3 * nsub]
            gnew = jnp.where(row_s == 2, rs[:, :SUPER], jnp.where(row_s == 3, rs[:, SUPER:], grv))
            grow[:, rows] = gnew
            cols.append((rows, _dot_tn(split3(gnew), sel)))
        for rows, col in cols:
            gcol[rows, :] = col
        return 0

    lax.fori_loop(0, ns // cum_unroll, cum_tile, 0)

    streams = ((q_ref, cwq_ref[...], qn, DN_HEAD_DIM ** -0.5), (k_ref, cwk_ref[...], kn, 1.0),
               (v_ref, cwv_ref[...], vn, None))
    for n in range(len(streams)):
        pad[n, 0:halo, :] = jnp.zeros((halo, DN_HEAD_DIM), F32)
        pad[n, T + halo:T + 2 * halo, :] = jnp.zeros((halo, DN_HEAD_DIM), F32)

    def fill(i, _):
        r = pl.multiple_of(i * rt, rt)
        for n, (src, _, _, _) in enumerate(streams):
            pad[n, pl.ds(r + halo, rt), :] = src[pl.ds(r, rt), :].astype(F32)
        oacc[pl.ds(r, rt), :] = jnp.zeros((rt, DN_HEAD_DIM), F32)
        return 0

    lax.fori_loop(0, nt, fill, 0, unroll=2)

    def conv_tile(i, _):
        base = i * rt + halo - CONV_WIDTH // 2
        for n, (_, cw, dst, scale) in enumerate(streams):
            acc = pad[n, pl.ds(base, rt), :] * cw[0:1, :]
            for j in range(1, CONV_WIDTH):
                acc = acc + pad[n, pl.ds(base + j, rt), :] * cw[j:j + 1, :]
            y = acc * jax.nn.sigmoid(acc)
            if scale is not None:
                y = y * (lax.rsqrt(jnp.sum(y * y, axis=-1, keepdims=True) + NORM_EPS) * scale)
            dst[pl.ds(pl.multiple_of(i * rt, rt), rt), :] = y
        return 0

    lax.fori_loop(0, nt, conv_tile, 0, unroll=2)

    s_f[...] = jnp.zeros((DN_HEAD_DIM, DN_HEAD_DIM), F32)
    s_b[...] = jnp.zeros((DN_HEAD_DIM, DN_HEAD_DIM), F32)
    fwd_row = lambda g, uu: pl.multiple_of((g * group + uu) * SUPER, SUPER)
    bwd_row = lambda g, uu: pl.multiple_of((ns - 1 - (g * group + uu)) * SUPER, SUPER)

    bufs = ((ap_f, nn_f, o0_f, et_f), (ap_b, nn_b, o0_b, et_b))

    def prepare(g):
        chains = [(d, (fwd_row, bwd_row)[d](g, uu), uu) for uu in range(group) for d in range(2)]
        return _prepare_stages(chains, qn, kn, vn, gcol, grow, msk, bufs)

    def scan(g):
        return _scan_stages([fwd_row(g, uu) for uu in range(group)], [bwd_row(g, uu) for uu in range(group)],
                            bufs, s_f, s_b, oacc)

    _interleave(prepare(0))

    def group_step(g, _):
        _interleave(prepare(g + 1), scan(g))
        return 0

    n_groups = ns // group
    lax.fori_loop(0, n_groups - 1, group_step, 0)
    _interleave(scan(n_groups - 1))

    nw = nw_ref[...]

    def out_tile(i, _):
        r = pl.ds(pl.multiple_of(i * rt, rt), rt)
        z = z_ref[r, :].astype(F32)
        o_ref[r, :] = (_rms(oacc[r, :], nw) * (z * jax.nn.sigmoid(z))).astype(o_ref.dtype)
        return 0

    lax.fori_loop(0, nt, out_tile, 0, unroll=2)


def _largest_divisor(n, cap):
    return max(d for d in range(1, cap + 1) if n % d == 0)


def _deltanet(y_dn, y_sr, conv_w, lr, nw):
    B, T, _ = y_dn.shape
    H, hd = DN_HEADS, DN_HEAD_DIM
    rt = min(256, T)
    ns = T // SUPER
    group = _largest_divisor(ns, DN_GROUP)
    assert 2 * group + 1 <= _PREPARE_STAGES_BEFORE_SLOT_WRITE
    col = lambda off: pl.BlockSpec((None, T, hd), lambda b, h: (b, 0, off + h))
    cw = lambda off: pl.BlockSpec((CONV_WIDTH, hd), lambda b, h: (0, off + h))
    tok = pltpu.VMEM((T, hd), F32)
    slots = [pltpu.VMEM((group, 2, hd + CHUNK, hd), BF16), pltpu.VMEM((group, 2, hd, hd), F32),
             pltpu.VMEM((group, SUPER, hd), F32), pltpu.VMEM((group, 2, V7X_SUBLANES, hd), F32)]
    return pl.pallas_call(
        functools.partial(_dn_body, T=T, rt=rt, group=group, cum_unroll=_largest_divisor(ns, 8)),
        grid=(B, H),
        in_specs=[col(0), col(H), col(2 * H), col(3 * H),
                  pl.BlockSpec((None, V7X_SUBLANES, T), lambda b, h: (b, h, 0)),
                  cw(0), cw(H), cw(2 * H),
                  pl.BlockSpec((None, V7X_SUBLANES, 2), lambda b, h: (h, 0, 0)),
                  pl.BlockSpec((1, hd), lambda b, h: (0, 0))],
        out_specs=pl.BlockSpec((None, T, hd), lambda b, h: (b, 0, h)),
        out_shape=jax.ShapeDtypeStruct((B, T, DN_WIDTH), BF16),
        scratch_shapes=[pltpu.VMEM((3, T + 2 * V7X_SUBLANES, hd), F32), tok, tok, tok,
                        pltpu.VMEM((T, V7X_LANES), F32), pltpu.VMEM((V7X_SUBLANES, T), F32), tok,
                        pltpu.VMEM((hd, hd), F32), pltpu.VMEM((hd, hd), F32),
                        pltpu.VMEM((_NMASK, SUPER, SUPER), F32)] + slots + slots,
        compiler_params=_params(("arbitrary", "arbitrary")),
        name="deltanet",
    )(y_dn, y_dn, y_dn, y_dn, y_sr, conv_w, conv_w, conv_w, lr, nw)


def _attn_body(sink_ref, q_ref, k_ref, v_ref, o_ref, k2, v2, bias, *, T, tq, rt):
    i = pl.program_id(1)
    nb = T // WINDOW
    half = ATTN_HEAD_DIM
    lane = lax.broadcasted_iota(jnp.int32, (WINDOW, V7X_LANES), 1)
    lo_mask = jnp.where(lane < half, 1.0, 0.0).astype(F32)
    hi_mask = 1.0 - lo_mask

    @pl.when(i == 0)
    def _init():
        zeros = jnp.zeros((WINDOW, V7X_LANES), BF16)
        for kv in range(ATTN_KV_HEADS):
            for r0 in (0, T + WINDOW):
                k2[kv, r0:r0 + WINDOW, :] = zeros
                v2[kv, r0:r0 + WINDOW, 0:V7X_LANES] = zeros
                v2[kv, r0:r0 + WINDOW, V7X_LANES:2 * V7X_LANES] = jnp.ones((WINDOW, V7X_LANES), BF16)
        lane_r = lax.broadcasted_iota(jnp.int32, (rt, V7X_LANES), 1)

        def fill(t, _):
            src = pl.ds(pl.multiple_of(t * rt, rt), rt)
            dst = pl.ds(pl.multiple_of(t * rt, rt) + WINDOW, rt)
            kf = k_ref[src, :].astype(F32)
            vf = v_ref[src, :].astype(F32)
            for kv in range(ATTN_KV_HEADS):
                sel = (lane_r < half) if kv == 0 else (lane_r >= half)
                km = jnp.where(sel, kf, 0.0)
                vm = jnp.where(sel, vf, 0.0)
                k2[kv, dst, :] = (km + pltpu.roll(km, half, 1)).astype(BF16)
                v2[kv, dst, 0:V7X_LANES] = (vm + pltpu.roll(vm, half, 1)).astype(BF16)
                v2[kv, dst, V7X_LANES:2 * V7X_LANES] = jnp.ones((rt, V7X_LANES), BF16)
            return 0

        lax.fori_loop(0, T // rt, fill, 0)
        a = lax.broadcasted_iota(jnp.int32, (WINDOW, 3 * WINDOW), 0)
        j = lax.broadcasted_iota(jnp.int32, (WINDOW, 3 * WINDOW), 1)
        dist = jnp.abs(j - WINDOW - a)
        for h in range(ATTN_Q_HEADS):
            slope = 2.0 ** (-8.0 * (h + 1) / ATTN_Q_HEADS)
            bias[h] = jnp.where(dist <= WINDOW, -slope * dist.astype(F32), MASKED)

    jrow = lax.broadcasted_iota(jnp.int32, (1, 3 * WINDOW), 1)

    def window(qi):
        return pl.ds(pl.multiple_of((i * (tq // WINDOW) + qi) * WINDOW, WINDOW), 3 * WINDOW)

    def score(qi, h):
        c0 = (h // 2) * V7X_LANES
        qf = q_ref[qi * WINDOW:(qi + 1) * WINDOW, c0:c0 + V7X_LANES].astype(F32) * (ATTN_HEAD_DIM ** -0.5)
        qm = (qf * (lo_mask, hi_mask)[h % 2]).astype(BF16)
        return _dot_nt(qm, k2[h // ATTN_GROUP, window(qi), :])

    def attend(qi, h, s, pos_bias):
        sink = sink_ref[h]
        s = s + bias[h] + pos_bias
        m = jnp.maximum(jnp.max(s, axis=-1, keepdims=True), sink)
        p = jnp.exp(s - m).astype(BF16)
        ov = _dot(p, v2[h // ATTN_GROUP, window(qi), :])
        return ov[:, :V7X_LANES] / (ov[:, V7X_LANES:] + jnp.exp(sink - m))

    nblk = tq // WINDOW
    scores = [score(0, h) for h in range(ATTN_Q_HEADS)]
    for qi in range(nblk):
        gi = i * nblk + qi
        first_valid = jnp.where(gi == 0, WINDOW, 0)
        end_valid = jnp.where(gi == nb - 1, 2 * WINDOW, 3 * WINDOW)
        outside = (jrow < first_valid) | (jrow >= end_valid)
        pos_bias = jnp.where(outside, MASKED, 0.0)
        tiles = []
        for h in range(ATTN_Q_HEADS):
            tiles.append(attend(qi, h, scores[h], pos_bias))
            if qi + 1 < nblk:
                scores[h] = score(qi + 1, h)
        for pair in range(ATTN_Q_HEADS // 2):
            c0 = pair * V7X_LANES
            o_ref[qi * WINDOW:(qi + 1) * WINDOW, c0:c0 + V7X_LANES] = \
                jnp.where(lane < half, tiles[2 * pair], tiles[2 * pair + 1]).astype(o_ref.dtype)


def _attention(y_at, sink, tq):
    B, T, _ = y_at.shape
    rt = min(256, T)
    return pl.pallas_call(
        functools.partial(_attn_body, T=T, tq=tq, rt=rt),
        grid=(B, T // tq),
        in_specs=[pl.BlockSpec(memory_space=pltpu.SMEM),
                  pl.BlockSpec((None, tq, ATTN_WIDTH), lambda b, i: (b, i, 0)),
                  pl.BlockSpec((None, T, KV_WIDTH), lambda b, i: (b, 0, ATTN_WIDTH // KV_WIDTH)),
                  pl.BlockSpec((None, T, KV_WIDTH), lambda b, i: (b, 0, ATTN_WIDTH // KV_WIDTH + 1))],
        out_specs=pl.BlockSpec((None, tq, ATTN_WIDTH), lambda b, i: (b, i, 0)),
        out_shape=jax.ShapeDtypeStruct((B, T, ATTN_WIDTH), BF16),
        scratch_shapes=[pltpu.VMEM((ATTN_KV_HEADS, T + 2 * WINDOW, V7X_LANES), BF16),
                        pltpu.VMEM((ATTN_KV_HEADS, T + 2 * WINDOW, 2 * V7X_LANES), BF16),
                        pltpu.VMEM((ATTN_Q_HEADS, WINDOW, 3 * WINDOW), F32)],
        compiler_params=_params(("arbitrary", "arbitrary")),
        name="window_attn",
    )(sink, y_at, y_at, y_at)


def _merge_body(x_ref, a_ref, b_ref, g_ref, pa_ref, pb_ref, wo_ref, gpost_ref, o_ref, *, d_model):
    ya = _dot(a_ref[...], pa_ref[...])
    yb = _dot(b_ref[...], pb_ref[...])
    g = g_ref[...].astype(F32)
    merged = jax.nn.sigmoid(g[:, :d_model]) * ya + jax.nn.sigmoid(g[:, d_model:]) * yb
    mix = _dot(merged.astype(BF16), wo_ref[...])
    o_ref[...] = x_ref[...] + _rms(mix, gpost_ref[...])


def _merge(x, out_a, out_b, y_g, pa, pb, wo, gpost, tm):
    B, T, D = x.shape
    tok = lambda n: pl.BlockSpec((None, tm, n), lambda b, i: (b, i, 0))
    return pl.pallas_call(
        functools.partial(_merge_body, d_model=D),
        grid=(B, T // tm),
        in_specs=[tok(D), tok(out_a.shape[-1]), tok(out_b.shape[-1]), tok(y_g.shape[-1]),
                  _resident(pa.shape), _resident(pb.shape), _resident(wo.shape), _resident((1, D))],
        out_specs=tok(D),
        out_shape=jax.ShapeDtypeStruct((B, T, D), F32),
        compiler_params=_params(("arbitrary", "arbitrary")),
        name="merge",
    )(x, out_a, out_b, y_g, pa, pb, wo, gpost)


def _split_w_in(w_in):
    D = w_in.shape[0]
    H = DN_HEADS
    o_small = 4 * DN_WIDTH
    o_attn = o_small + 4 * H
    o_gate = o_attn + ATTN_WIDTH + 2 * KV_WIDTH
    wdn = w_in[:, :o_small].astype(BF16)
    small = w_in[:, o_small:o_attn].reshape(D, 4, H)
    small = jnp.transpose(small, (0, 2, 1))
    wsr =jnp.pad(small, ((0, 0), (0, 0), (0, V7X_SUBLANES - 4))).reshape(D, H * V7X_SUBLANES).T.astype(BF16)
    wat = w_in[:, o_attn:o_gate].astype(BF16)
    wg = w_in[:, o_gate:].astype(BF16)
    return wdn, wsr, wat, wg


def _gate_consts(a_log, dt_bias):
    H = DN_HEADS
    both = jnp.stack([a_log, dt_bias], axis=0).astype(F32)
    per_head = jnp.transpose(both, (2, 1, 0))
    return jnp.zeros((H, V7X_SUBLANES, 2), F32).at[:, 2:4, :].set(per_head)


def _tile(n, pref):
    t = min(pref, n)
    assert n % t == 0
    return t


def _trunk(x, layers):
    T = x.shape[1]
    tm = _tile(T, 512)
    for p in layers:
        x = _ffn(x, p["f1_pre"], p["f1_post"], p["f1_w_in"], p["f1_w_out"], tm)
        y_dn, y_sr, y_at, y_g = _proj(x, p["m_pre"], p["wdn"], p["wsr"], p["wat"], p["wg"], tm)
        out_a = _deltanet(y_dn, y_sr, p["conv_w"], p["lr"], p["dn_norm_w"])
        out_b = _attention(y_at, p["sink"], tm)
        x = _merge(x, out_a, out_b, y_g, p["pa"], p["pb"], p["wo"], p["m_post"], tm)
        x = _ffn(x, p["f2_pre"], p["f2_post"], p["f2_w_in"], p["f2_w_out"], tm)
    return x


def kernel(x_prompt, x_sample, ffn1_norm_pre, ffn1_norm_post, ffn1_w_in, ffn1_w_out, mix_norm_pre, mix_norm_post, mix_w_in, dn_conv_w, dn_a_log, dn_dt_bias, dn_norm_w, attn_sink, w_branch_a, w_branch_b, mix_w_out, ffn2_norm_pre, ffn2_norm_post, ffn2_w_in, ffn2_w_out):
    depth = ffn1_w_in.shape[0]
    row = lambda v: v.reshape(1, -1).astype(F32)
    layers = []
    for l in range(depth):
        wdn, wsr, wat, wg = _split_w_in(mix_w_in[l])
        lr = _gate_consts(dn_a_log[l], dn_dt_bias[l])
        layers.append(dict(
            f1_pre=row(ffn1_norm_pre[l]), f1_post=row(ffn1_norm_post[l]),
            f1_w_in=ffn1_w_in[l].astype(BF16), f1_w_out=ffn1_w_out[l].astype(BF16),
            m_pre=row(mix_norm_pre[l]), m_post=row(mix_norm_post[l]),
            wdn=wdn, wsr=wsr, wat=wat, wg=wg,
            conv_w=dn_conv_w[l].astype(F32), lr=lr, dn_norm_w=row(dn_norm_w[l]),
            sink=attn_sink[l].astype(F32),
            pa=w_branch_a[l].astype(BF16), pb=w_branch_b[l].astype(BF16), wo=mix_w_out[l].astype(BF16),
            f2_pre=row(ffn2_norm_pre[l]), f2_post=row(ffn2_norm_post[l]),
            f2_w_in=ffn2_w_in[l].astype(BF16), f2_w_out=ffn2_w_out[l].astype(BF16),
        ))
    return (_trunk(x_prompt, layers), _trunk(x_sample, layers))
```

```python
import functools

import jax
import jax.numpy as jnp
from jax import lax
from jax.experimental import pallas as pl
from jax.experimental.pallas import tpu as pltpu

F32 = jnp.float32
BF16 = jnp.bfloat16
HIGHEST = lax.Precision.HIGHEST

NORM_EPS = 1e-6
DN_HEADS = 4
DN_HEAD_DIM = 128
DN_WIDTH = DN_HEADS * DN_HEAD_DIM
CONV_WIDTH = 5
ATTN_Q_HEADS = 8
ATTN_KV_HEADS = 2
ATTN_GROUP = ATTN_Q_HEADS // ATTN_KV_HEADS
ATTN_HEAD_DIM = 64
ATTN_WIDTH = ATTN_Q_HEADS * ATTN_HEAD_DIM
KV_WIDTH = ATTN_KV_HEADS * ATTN_HEAD_DIM
WINDOW = 128

V7X_LANES = 128
V7X_SUBLANES = 8
V7X_VMEM_LIMIT_BYTES = 56 * 1024 * 1024

SUPER = 128
DN_GROUP = 8
assert SUPER == V7X_LANES and WINDOW == V7X_LANES and DN_HEAD_DIM == V7X_LANES
MASKED = -1e30

_EYE, _TRI0, _SREL0, _M8_0 = 0, 1, 3, 5
_MERGE_LEVELS = (3, 4, 5, 6)
_MERGE0 = 7
_NMASK = _MERGE0 + 2 * len(_MERGE_LEVELS)
_PREPARE_STAGES_BEFORE_SLOT_WRITE = 13


def _rms(x, g):
    return x * lax.rsqrt(jnp.mean(x * x, axis=-1, keepdims=True) + NORM_EPS) * g


def _dot(a, b):
    return jnp.dot(a, b, preferred_element_type=F32)


def _dot_nt(a, b):
    return lax.dot_general(a, b, (((1,), (1,)), ((), ())), preferred_element_type=F32)


def _dot_tn(a, b):
    return lax.dot_general(a, b, (((0,), (0,)), ((), ())), preferred_element_type=F32)


def _softplus(x):
    return jnp.maximum(x, 0.0) + jnp.log1p(jnp.exp(-jnp.abs(x)))


def _resident(shape):
    nd = len(shape)
    return pl.BlockSpec(shape, lambda *_: (0,) * nd, pipeline_mode=pl.Buffered(1))


def _params(sem):
    return pltpu.CompilerParams(dimension_semantics=sem, vmem_limit_bytes=V7X_VMEM_LIMIT_BYTES)


def _ffn_body(x_ref, gpre_ref, gpost_ref, win_ref, wout_ref, o_ref, *, d_ff):
    x = x_ref[...]
    h = _rms(x, gpre_ref[...]).astype(BF16)
    gu = _dot(h, win_ref[...])
    gate = gu[:, :d_ff]
    up = gu[:, d_ff:]
    act = (gate * jax.nn.sigmoid(gate) * up).astype(BF16)
    y = _dot(act, wout_ref[...])
    o_ref[...] = x + 0.5 * _rms(y, gpost_ref[...])


def _ffn(x, gpre, gpost, w_in, w_out, tm):
    B, T, D = x.shape
    d_ff = w_out.shape[0]
    x2 = x.reshape(B * T, D)
    out = pl.pallas_call(
        functools.partial(_ffn_body, d_ff=d_ff),
        grid=(B * T // tm,),
        in_specs=[
            pl.BlockSpec((tm, D), lambda i: (i, 0)),
            _resident((1, D)),
            _resident((1, D)),
            _resident((D, 2 * d_ff)),
            _resident((d_ff, D)),
        ],
        out_specs=pl.BlockSpec((tm, D), lambda i: (i, 0)),
        out_shape=jax.ShapeDtypeStruct((B * T, D), F32),
        compiler_params=_params(("arbitrary",)),
        name="ffn",
    )(x2, gpre, gpost, w_in, w_out)
    return out.reshape(B, T, D)


def _proj_body(x_ref, g_ref, wdn_ref, wsr_ref, wat_ref, wg_ref, odn_ref, osr_ref, oat_ref, og_ref):
    u = _rms(x_ref[...], g_ref[...]).astype(BF16)
    odn_ref[...] = _dot(u, wdn_ref[...]).astype(odn_ref.dtype)
    osr_ref[...] = _dot_nt(wsr_ref[...], u)
    oat_ref[...] = _dot(u, wat_ref[...]).astype(oat_ref.dtype)
    og_ref[...] = _dot(u, wg_ref[...]).astype(og_ref.dtype)


def _proj(x, g, wdn, wsr, wat, wg, tm):
    B, T, D = x.shape
    tok = lambda n: pl.BlockSpec((None, tm, n), lambda b, i: (b, i, 0))
    nsr = wsr.shape[0]
    return pl.pallas_call(
        _proj_body,
        grid=(B, T // tm),
        in_specs=[tok(D), _resident((1, D)), _resident(wdn.shape),
                  _resident(wsr.shape), _resident(wat.shape), _resident(wg.shape)],
        out_specs=[tok(wdn.shape[1]),
                   pl.BlockSpec((None, nsr, tm), lambda b, i: (b, 0, i)),
                   tok(wat.shape[1]), tok(wg.shape[1])],
        out_shape=[
            jax.ShapeDtypeStruct((B, T, wdn.shape[1]), BF16),
            jax.ShapeDtypeStruct((B, nsr, T), F32),
            jax.ShapeDtypeStruct((B, T, wat.shape[1]), BF16),
            jax.ShapeDtypeStruct((B, T, wg.shape[1]), BF16),
        ],
        compiler_params=_params(("arbitrary", "arbitrary")),
        name="proj",
    )(x, g, wdn, wsr, wat, wg)


def _build_masks(msk):
    ii = lax.broadcasted_iota(jnp.int32, (SUPER, SUPER), 0)
    jj = lax.broadcasted_iota(jnp.int32, (SUPER, SUPER), 1)
    same = lambda s: jnp.right_shift(ii, s) == jnp.right_shift(jj, s)
    f = lambda c: jnp.where(c, 1.0, 0.0).astype(F32)
    msk[_EYE] = f(ii == jj)
    for d in range(2):
        after = (ii > jj) if d == 0 else (ii < jj)
        after_eq = (ii >= jj) if d == 0 else (ii <= jj)
        msk[_TRI0 + d] = f(after_eq)
        msk[_SREL0 + d] = f(after)
        msk[_M8_0 + d] = f(same(_MERGE_LEVELS[0]) & after)
        for n, lvl in enumerate(_MERGE_LEVELS):
            msk[_MERGE0 + 2 * n + d] = f(same(lvl + 1) & jnp.logical_not(same(lvl)) & after)


def _prepare_stages(chains, qn, kn, vn, gcol, grow, msk, bufs):
    mm = lambda a, b: _dot(a.astype(BF16), b.astype(BF16))
    eye = msk[_EYE]
    st = []
    for d, r0, slot in chains:
        rows = pl.ds(r0, SUPER)
        c = dict(d=d, slot=slot, rows=rows)
        c["kb"] = kn[rows, :].astype(BF16)
        c["vb"] = vn[rows, :].astype(BF16)
        grv = grow[:, rows]
        c["gc_r"] = grv[2 + d:3 + d, :]
        c["beta_r"] = grv[d:d + 1, :]
        c["qg"] = _dot_nt(jnp.concatenate([qn[rows, :].astype(BF16), c["kb"]], axis=0), c["kb"])
        st.append(c)
    yield
    for c in st:
        srel = msk[_SREL0 + c["d"]]
        dec = jnp.exp((gcol[2 + c["d"], c["rows"], :] - c["gc_r"]) * srel) * srel
        c["lm"] = c["qg"][SUPER:] * gcol[c["d"], c["rows"], :] * dec
        c["intra"] = (c["qg"][:SUPER] * (dec + eye)).astype(BF16)
        c["x"] = c["lm"] * msk[_M8_0 + c["d"]]
        c["x2"] = mm(c["x"], c["x"])
    yield
    for c in st:
        x34 = mm(c["x2"], jnp.concatenate([c["x"], c["x2"]], axis=1))
        c["x3"] = x34[:, :SUPER]
        c["x4"] = x34[:, SUPER:]
    yield
    for c in st:
        p1 = eye - c["x"] + c["x2"] - c["x3"]
        c["inv"] = p1 + mm(p1, c["x4"])
    yield
    for n in range(len(_MERGE_LEVELS)):
        for c in st:
            c["t"] = mm(c["inv"], c["lm"] * msk[_MERGE0 + 2 * n + c["d"]])
        yield
        for c in st:
            c["inv"] = c["inv"] - mm(c["t"], c["inv"])
        yield
    for c in st:
        tb = c["inv"] * c["beta_r"]
        tbe = tb * jnp.exp(c["gc_r"])
        u = _dot(tb.astype(BF16), c["vb"])
        w = _dot(tbe.astype(BF16), c["kb"])
        c["wu"] = jnp.concatenate([w, u], axis=1).astype(BF16)
    yield
    for c in st:
        ap, nn, o0b, et = bufs[c["d"]]
        iw = _dot(c["intra"], c["wu"])
        gc_c = gcol[2 + c["d"], c["rows"], :]
        c["qp"] = (qn[c["rows"], :] * jnp.exp(gc_c) - iw[:, :DN_HEAD_DIM]).astype(BF16)
        o0b[c["slot"]] = iw[:, DN_HEAD_DIM:]
    yield
    for c in st:
        ap, nn, o0b, et = bufs[c["d"]]
        last = SUPER - 1 if c["d"] == 0 else 0
        gc_c = gcol[2 + c["d"], c["rows"], :]
        tot = gc_c[last:last + 1, :]
        kdec = (kn[c["rows"], :] * jnp.exp(tot - gc_c)).astype(BF16)
        mn = _dot_tn(kdec, c["wu"])
        ap[c["slot"]] = jnp.concatenate([mn[:, :DN_HEAD_DIM].astype(BF16), c["qp"]], axis=0)
        nn[c["slot"]] = mn[:, DN_HEAD_DIM:]
        et[c["slot"]] = jnp.broadcast_to(jnp.exp(tot), (V7X_SUBLANES, DN_HEAD_DIM))
    yield


def _scan_stages(rows_f, rows_b, bufs, s_f, s_b, oacc):
    state = [s_f[...], s_b[...]]
    for slot, r0s in enumerate(zip(rows_f, rows_b)):
        for d in range(2):
            ap, nn, o0b, et = bufs[d]
            s = state[d]
            r = _dot(ap[slot], s.astype(BF16))
            oacc[pl.ds(r0s[d], SUPER), :] += r[DN_HEAD_DIM:] + o0b[slot]
            s3 = s.reshape(DN_HEAD_DIM // V7X_SUBLANES, V7X_SUBLANES, DN_HEAD_DIM) * et[slot][None]
            state[d] = s3.reshape(DN_HEAD_DIM, DN_HEAD_DIM) - r[:DN_HEAD_DIM] + nn[slot]
        yield
    s_f[...] = state[0]
    s_b[...] = state[1]
    yield


def _interleave(*stage_generators):
    live = list(stage_generators)
    while live:
        for gen in list(live):
            try:
                next(gen)
            except StopIteration:
                live.remove(gen)


def _dn_body(q_ref, k_ref, v_ref, z_ref, gr_ref, cwq_ref, cwk_ref, cwv_ref, lr_ref, nw_ref, o_ref,
             pad, qn, kn, vn, gcol, grow, oacc, s_f, s_b, msk,
             ap_f, nn_f, o0_f, et_f, ap_b, nn_b, o0_b, et_b, *, T, rt, group, cum_unroll):
    nt = T // rt
    ns = T // SUPER
    halo = V7X_SUBLANES
    nsub = V7X_SUBLANES
    _build_masks(msk)

    xr = gr_ref[...]
    lr = lr_ref[...]
    rowi = lax.broadcasted_iota(jnp.int32, xr.shape, 0)
    grow[...] = jnp.where(rowi < 2, jax.nn.sigmoid(xr),
                          -jnp.exp(lr[:, 0:1]) * _softplus(xr + lr[:, 1:2]))

    ngate = 4
    tri_bf = jnp.concatenate([msk[_TRI0 + 1], msk[_TRI0]], axis=1).astype(BF16)
    row_s = lax.broadcasted_iota(jnp.int32, (nsub, SUPER), 0)
    sel_r = lax.broadcasted_iota(jnp.int32, (4 * nsub, ngate * V7X_LANES), 0)
    sel_n = lax.broadcasted_iota(jnp.int32, (4 * nsub, ngate * V7X_LANES), 1)
    sel = jnp.where(((sel_r & (nsub - 1)) == jnp.right_shift(sel_n, 7)) & (sel_r < 3 * nsub), 1.0, 0.0).astype(BF16)

    def split3(x):
        hi = x.astype(BF16).astype(F32)
        r1 = x - hi
        mid = r1.astype(BF16).astype(F32)
        lo = r1 - mid
        return jnp.concatenate([hi, mid, lo, jnp.zeros_like(x)], axis=0).astype(BF16)

    def cum_tile(i, _):
        tiles = []
        for uu in range(cum_unroll):
            rows = pl.ds(pl.multiple_of((i * cum_unroll + uu) * SUPER, SUPER), SUPER)
            grv = grow[:, rows]
            tiles.append((rows, grv, _dot(split3(grv), tri_bf)))
        cols = []
        for rows, grv, rs in tiles:
            rs = rs[:nsub] + rs[nsub:2 * nsub] + rs[2 * nsub:3 * nsub]
            gnew = jnp.where(row_s == 2, rs[:, :SUPER], jnp.where(row_s == 3, rs[:, SUPER:], grv))
            grow[:, rows] = gnew
            cols.append((rows, _dot_tn(split3(gnew), sel)))
        for rows, col in cols:
            for j in range(ngate):
                gcol[j, rows, :] = col[:, j * V7X_LANES:(j + 1) * V7X_LANES]
        return 0

    lax.fori_loop(0, ns // cum_unroll, cum_tile, 0)

    streams = ((q_ref, cwq_ref[...], qn, DN_HEAD_DIM ** -0.5), (k_ref, cwk_ref[...], kn, 1.0),
               (v_ref, cwv_ref[...], vn, None))
    for n in range(len(streams)):
        pad[n, 0:halo, :] = jnp.zeros((halo, DN_HEAD_DIM), F32)
        pad[n, T + halo:T + 2 * halo, :] = jnp.zeros((halo, DN_HEAD_DIM), F32)

    def fill(i, _):
        r = pl.multiple_of(i * rt, rt)
        for n, (src, _, _, _) in enumerate(streams):
            pad[n, pl.ds(r + halo, rt), :] = src[pl.ds(r, rt), :].astype(F32)
        oacc[pl.ds(r, rt), :] = jnp.zeros((rt, DN_HEAD_DIM), F32)
        return 0

    lax.fori_loop(0, nt, fill, 0, unroll=2)

    def conv_rows(r0):
        base = r0 + halo - CONV_WIDTH // 2
        for n, (_, cw, dst, scale) in enumerate(streams):
            acc = pad[n, pl.ds(base, rt), :] * cw[0:1, :]
            for j in range(1, CONV_WIDTH):
                acc = acc + pad[n, pl.ds(base + j, rt), :] * cw[j:j + 1, :]
            y = acc * jax.nn.sigmoid(acc)
            if scale is not None:
                y = y * (lax.rsqrt(jnp.sum(y * y, axis=-1, keepdims=True) + NORM_EPS) * scale)
            dst[pl.ds(r0, rt), :] = y

    nw = nw_ref[...]

    def out_rows(r0):
        r = pl.ds(r0, rt)
        z = z_ref[r, :].astype(F32)
        o_ref[r, :] = (_rms(oacc[r, :], nw) * (z * jax.nn.sigmoid(z))).astype(o_ref.dtype)

    def loop_rows(fn, lo, hi):
        def tile(i, _):
            fn(pl.multiple_of(i * rt, rt))
            return 0
        lax.fori_loop(lo // rt, hi // rt, tile, 0, unroll=2)

    def row_stages(fn, lo, hi):
        for r0 in range(lo, hi, rt):
            fn(r0)
            yield

    s_f[...] = jnp.zeros((DN_HEAD_DIM, DN_HEAD_DIM), F32)
    s_b[...] = jnp.zeros((DN_HEAD_DIM, DN_HEAD_DIM), F32)
    fwd_row = lambda g, uu: pl.multiple_of((g * group + uu) * SUPER, SUPER)
    bwd_row = lambda g, uu: pl.multiple_of((ns - 1 - (g * group + uu)) * SUPER, SUPER)

    bufs = ((ap_f, nn_f, o0_f, et_f), (ap_b, nn_b, o0_b, et_b))

    def prepare(g):
        chains = [(d, (fwd_row, bwd_row)[d](g, uu), uu) for uu in range(group) for d in range(2)]
        return _prepare_stages(chains, qn, kn, vn, gcol, grow, msk, bufs)

    def scan(g):
        return _scan_stages([fwd_row(g, uu) for uu in range(group)], [bwd_row(g, uu) for uu in range(group)],
                            bufs, s_f, s_b, oacc)

    n_groups = ns // group
    edge = group * SUPER
    assert edge % rt == 0
    loop_rows(conv_rows, 0, edge)
    if n_groups >= 2:
        loop_rows(conv_rows, T - edge, T)
    _interleave(prepare(0), row_stages(conv_rows, edge, T - edge))

    def group_step(g, _):
        _interleave(prepare(g + 1), scan(g))
        return 0

    lax.fori_loop(0, n_groups - 1, group_step, 0)
    _interleave(scan(n_groups - 1), row_stages(out_rows, edge, T - edge))
    loop_rows(out_rows, 0, edge)
    if n_groups >= 2:
        loop_rows(out_rows, T - edge, T)


def _largest_divisor(n, cap):
    return max(d for d in range(1, cap + 1) if n % d == 0)


def _deltanet(y_dn, y_sr, conv_w, lr, nw):
    B, T, _ = y_dn.shape
    H, hd = DN_HEADS, DN_HEAD_DIM
    rt = min(256, T)
    ns = T // SUPER
    group = _largest_divisor(ns, DN_GROUP)
    assert group + 1 <= _PREPARE_STAGES_BEFORE_SLOT_WRITE
    col = lambda off: pl.BlockSpec((None, T, hd), lambda b, h: (b, 0, off + h))
    cw = lambda off: pl.BlockSpec((CONV_WIDTH, hd), lambda b, h: (0, off + h))
    tok = pltpu.VMEM((T, hd), F32)
    slots = [pltpu.VMEM((group, hd + SUPER, hd), BF16), pltpu.VMEM((group, hd, hd), F32),
             pltpu.VMEM((group, SUPER, hd), F32), pltpu.VMEM((group, V7X_SUBLANES, hd), F32)]
    return pl.pallas_call(
        functools.partial(_dn_body, T=T, rt=rt, group=group, cum_unroll=_largest_divisor(ns, 8)),
        grid=(B, H),
        in_specs=[col(0), col(H), col(2 * H), col(3 * H),
                  pl.BlockSpec((None, V7X_SUBLANES, T), lambda b, h: (b, h, 0)),
                  cw(0), cw(H), cw(2 * H),
                  pl.BlockSpec((None, V7X_SUBLANES, 2), lambda b, h: (h, 0, 0)),
                  pl.BlockSpec((1, hd), lambda b, h: (0, 0))],
        out_specs=pl.BlockSpec((None, T, hd), lambda b, h: (b, 0, h)),
        out_shape=jax.ShapeDtypeStruct((B, T, DN_WIDTH), BF16),
        scratch_shapes=[pltpu.VMEM((3, T + 2 * V7X_SUBLANES, hd), F32), tok, tok, tok,
                        pltpu.VMEM((4, T, V7X_LANES), F32), pltpu.VMEM((V7X_SUBLANES, T), F32), tok,
                        pltpu.VMEM((hd, hd), F32), pltpu.VMEM((hd, hd), F32),
                        pltpu.VMEM((_NMASK, SUPER, SUPER), F32)] + slots + slots,
        compiler_params=_params(("arbitrary", "arbitrary")),
        name="deltanet",
    )(y_dn, y_dn, y_dn, y_dn, y_sr, conv_w, conv_w, conv_w, lr, nw)


def _attn_body(sink_ref, q_ref, k_ref, v_ref, o_ref, k2, v2, bias, *, T, tq, rt):
    i = pl.program_id(1)
    nb = T // WINDOW
    half = ATTN_HEAD_DIM
    lane = lax.broadcasted_iota(jnp.int32, (WINDOW, V7X_LANES), 1)
    lo_mask = jnp.where(lane < half, 1.0, 0.0).astype(F32)
    hi_mask = 1.0 - lo_mask

    @pl.when(i == 0)
    def _init():
        zeros = jnp.zeros((WINDOW, V7X_LANES), BF16)
        for kv in range(ATTN_KV_HEADS):
            for r0 in (0, T + WINDOW):
                k2[kv, r0:r0 + WINDOW, :] = zeros
                v2[kv, r0:r0 + WINDOW, 0:V7X_LANES] = zeros
                v2[kv, r0:r0 + WINDOW, V7X_LANES:2 * V7X_LANES] = jnp.ones((WINDOW, V7X_LANES), BF16)
        lane_r = lax.broadcasted_iota(jnp.int32, (rt, V7X_LANES), 1)

        def fill(t, _):
            src = pl.ds(pl.multiple_of(t * rt, rt), rt)
            dst = pl.ds(pl.multiple_of(t * rt, rt) + WINDOW, rt)
            kf = k_ref[src, :].astype(F32)
            vf = v_ref[src, :].astype(F32)
            for kv in range(ATTN_KV_HEADS):
                sel = (lane_r < half) if kv == 0 else (lane_r >= half)
                km = jnp.where(sel, kf, 0.0)
                vm = jnp.where(sel, vf, 0.0)
                k2[kv, dst, :] = (km + pltpu.roll(km, half, 1)).astype(BF16)
                v2[kv, dst, 0:V7X_LANES] = (vm + pltpu.roll(vm, half, 1)).astype(BF16)
                v2[kv, dst, V7X_LANES:2 * V7X_LANES] = jnp.ones((rt, V7X_LANES), BF16)
            return 0

        lax.fori_loop(0, T // rt, fill, 0)
        a = lax.broadcasted_iota(jnp.int32, (WINDOW, 3 * WINDOW), 0)
        j = lax.broadcasted_iota(jnp.int32, (WINDOW, 3 * WINDOW), 1)
        dist = jnp.abs(j - WINDOW - a)
        for h in range(ATTN_Q_HEADS):
            slope = 2.0 ** (-8.0 * (h + 1) / ATTN_Q_HEADS)
            bias[h] = jnp.where(dist <= WINDOW, -slope * dist.astype(F32), MASKED)

    jrow = lax.broadcasted_iota(jnp.int32, (1, 3 * WINDOW), 1)

    def window(qi):
        return pl.ds(pl.multiple_of((i * (tq // WINDOW) + qi) * WINDOW, WINDOW), 3 * WINDOW)

    def score(qi, h):
        c0 = (h // 2) * V7X_LANES
        qf = q_ref[qi * WINDOW:(qi + 1) * WINDOW, c0:c0 + V7X_LANES].astype(F32) * (ATTN_HEAD_DIM ** -0.5)
        qm = (qf * (lo_mask, hi_mask)[h % 2]).astype(BF16)
        return _dot_nt(qm, k2[h // ATTN_GROUP, window(qi), :])

    def attend(qi, h, s, pos_bias):
        sink = sink_ref[h]
        s = s + bias[h] + pos_bias
        m = jnp.maximum(jnp.max(s, axis=-1, keepdims=True), sink)
        p = jnp.exp(s - m).astype(BF16)
        ov = _dot(p, v2[h // ATTN_GROUP, window(qi), :])
        return ov[:, :V7X_LANES] / (ov[:, V7X_LANES:] + jnp.exp(sink - m))

    nblk = tq // WINDOW
    scores = [score(0, h) for h in range(ATTN_Q_HEADS)]
    for qi in range(nblk):
        gi = i * nblk + qi
        first_valid = jnp.where(gi == 0, WINDOW, 0)
        end_valid = jnp.where(gi == nb - 1, 2 * WINDOW, 3 * WINDOW)
        outside = (jrow < first_valid) | (jrow >= end_valid)
        pos_bias = jnp.where(outside, MASKED, 0.0)
        tiles = []
        for h in range(ATTN_Q_HEADS):
            tiles.append(attend(qi, h, scores[h], pos_bias))
            if qi + 1 < nblk:
                scores[h] = score(qi + 1, h)
        for pair in range(ATTN_Q_HEADS // 2):
            c0 = pair * V7X_LANES
            o_ref[qi * WINDOW:(qi + 1) * WINDOW, c0:c0 + V7X_LANES] = \
                jnp.where(lane < half, tiles[2 * pair], tiles[2 * pair + 1]).astype(o_ref.dtype)


def _attention(y_at, sink, tq):
    B, T, _ = y_at.shape
    rt = min(256, T)
    return pl.pallas_call(
        functools.partial(_attn_body, T=T, tq=tq, rt=rt),
        grid=(B, T // tq),
        in_specs=[pl.BlockSpec(memory_space=pltpu.SMEM),
                  pl.BlockSpec((None, tq, ATTN_WIDTH), lambda b, i: (b, i, 0)),
                  pl.BlockSpec((None, T, KV_WIDTH), lambda b, i: (b, 0, ATTN_WIDTH // KV_WIDTH)),
                  pl.BlockSpec((None, T, KV_WIDTH), lambda b, i: (b, 0, ATTN_WIDTH // KV_WIDTH + 1))],
        out_specs=pl.BlockSpec((None, tq, ATTN_WIDTH), lambda b, i: (b, i, 0)),
        out_shape=jax.ShapeDtypeStruct((B, T, ATTN_WIDTH), BF16),
        scratch_shapes=[pltpu.VMEM((ATTN_KV_HEADS, T + 2 * WINDOW, V7X_LANES), BF16),
                        pltpu.VMEM((ATTN_KV_HEADS, T + 2 * WINDOW, 2 * V7X_LANES), BF16),
                        pltpu.VMEM((ATTN_Q_HEADS, WINDOW, 3 * WINDOW), F32)],
        compiler_params=_params(("arbitrary", "arbitrary")),
        name="window_attn",
    )(sink, y_at, y_at, y_at)


def _merge_body(x_ref, a_ref, b_ref, g_ref, pa_ref, pb_ref, wo_ref, gpost_ref, o_ref, *, d_model):
    ya = _dot(a_ref[...], pa_ref[...])
    yb = _dot(b_ref[...], pb_ref[...])
    g = g_ref[...].astype(F32)
    merged = jax.nn.sigmoid(g[:, :d_model]) * ya + jax.nn.sigmoid(g[:, d_model:]) * yb
    mix = _dot(merged.astype(BF16), wo_ref[...])
    o_ref[...] = x_ref[...] + _rms(mix, gpost_ref[...])


def _merge(x, out_a, out_b, y_g, pa, pb, wo, gpost, tm):
    B, T, D = x.shape
    tok = lambda n: pl.BlockSpec((None, tm, n), lambda b, i: (b, i, 0))
    return pl.pallas_call(
        functools.partial(_merge_body, d_model=D),
        grid=(B, T // tm),
        in_specs=[tok(D), tok(out_a.shape[-1]), tok(out_b.shape[-1]), tok(y_g.shape[-1]),
                  _resident(pa.shape), _resident(pb.shape), _resident(wo.shape), _resident((1, D))],
        out_specs=tok(D),
        out_shape=jax.ShapeDtypeStruct((B, T, D), F32),
        compiler_params=_params(("arbitrary", "arbitrary")),
        name="merge",
    )(x, out_a, out_b, y_g, pa, pb, wo, gpost)


def _split_w_in(w_in):
    D = w_in.shape[0]
    H = DN_HEADS
    o_small = 4 * DN_WIDTH
    o_attn = o_small + 4 * H
    o_gate = o_attn + ATTN_WIDTH + 2 * KV_WIDTH
    wdn = w_in[:, :o_small].astype(BF16)
    small = w_in[:, o_small:o_attn].reshape(D, 4, H)
    small = jnp.transpose(small, (0, 2, 1))
    wsr =jnp.pad(small, ((0, 0), (0, 0), (0, V7X_SUBLANES - 4))).reshape(D, H * V7X_SUBLANES).T.astype(BF16)
    wat = w_in[:, o_attn:o_gate].astype(BF16)
    wg = w_in[:, o_gate:].astype(BF16)
    return wdn, wsr, wat, wg


def _gate_consts(a_log, dt_bias):
    H = DN_HEADS
    both = jnp.stack([a_log, dt_bias], axis=0).astype(F32)
    per_head = jnp.transpose(both, (2, 1, 0))
    return jnp.zeros((H, V7X_SUBLANES, 2), F32).at[:, 2:4, :].set(per_head)


def _tile(n, pref):
    t = min(pref, n)
    assert n % t == 0
    return t


def _trunk(x, layers):
    T = x.shape[1]
    tm = _tile(T, 512)
    for p in layers:
        x = _ffn(x, p["f1_pre"], p["f1_post"], p["f1_w_in"], p["f1_w_out"], tm)
        y_dn, y_sr, y_at, y_g = _proj(x, p["m_pre"], p["wdn"], p["wsr"], p["wat"], p["wg"], tm)
        out_a = _deltanet(y_dn, y_sr, p["conv_w"], p["lr"], p["dn_norm_w"])
        out_b = _attention(y_at, p["sink"], tm)
        x = _merge(x, out_a, out_b, y_g, p["pa"], p["pb"], p["wo"], p["m_post"], tm)
        x = _ffn(x, p["f2_pre"], p["f2_post"], p["f2_w_in"], p["f2_w_out"], tm)
    return x


def kernel(x_prompt, x_sample, ffn1_norm_pre, ffn1_norm_post, ffn1_w_in, ffn1_w_out, mix_norm_pre, mix_norm_post, mix_w_in, dn_conv_w, dn_a_log, dn_dt_bias, dn_norm_w, attn_sink, w_branch_a, w_branch_b, mix_w_out, ffn2_norm_pre, ffn2_norm_post, ffn2_w_in, ffn2_w_out):
    depth = ffn1_w_in.shape[0]
    row = lambda v: v.reshape(1, -1).astype(F32)
    layers = []
    for l in range(depth):
        wdn, wsr, wat, wg = _split_w_in(mix_w_in[l])
        lr = _gate_consts(dn_a_log[l], dn_dt_bias[l])
        layers.append(dict(
            f1_pre=row(ffn1_norm_pre[l]), f1_post=row(ffn1_norm_post[l]),
            f1_w_in=ffn1_w_in[l].astype(BF16), f1_w_out=ffn1_w_out[l].astype(BF16),
            m_pre=row(mix_norm_pre[l]), m_post=row(mix_norm_post[l]),
            wdn=wdn, wsr=wsr, wat=wat, wg=wg,
            conv_w=dn_conv_w[l].astype(F32), lr=lr, dn_norm_w=row(dn_norm_w[l]),
            sink=attn_sink[l].astype(F32),
            pa=w_branch_a[l].astype(BF16), pb=w_branch_b[l].astype(BF16), wo=mix_w_out[l].astype(BF16),
            f2_pre=row(ffn2_norm_pre[l]), f2_post=row(ffn2_norm_post[l]),
            f2_w_in=ffn2_w_in[l].astype(BF16), f2_w_out=ffn2_w_out[l].astype(BF16),
        ))
    return (_trunk(x_prompt, layers), _trunk(x_sample, layers))
```

```python
import functools

import jax
import jax.numpy as jnp
from jax import lax
from jax.experimental import pallas as pl
from jax.experimental.pallas import tpu as pltpu

F32 = jnp.float32
BF16 = jnp.bfloat16
HIGHEST = lax.Precision.HIGHEST

NORM_EPS = 1e-6
DN_HEADS = 4
DN_HEAD_DIM = 128
DN_WIDTH = DN_HEADS * DN_HEAD_DIM
CONV_WIDTH = 5
ATTN_Q_HEADS = 8
ATTN_KV_HEADS = 2
ATTN_GROUP = ATTN_Q_HEADS // ATTN_KV_HEADS
ATTN_HEAD_DIM = 64
ATTN_WIDTH = ATTN_Q_HEADS * ATTN_HEAD_DIM
KV_WIDTH = ATTN_KV_HEADS * ATTN_HEAD_DIM
WINDOW = 128

V7X_LANES = 128
V7X_SUBLANES = 8
V7X_VMEM_LIMIT_BYTES = 56 * 1024 * 1024
DENSE_TILE_ROWS = 1024
ATTN_TILE_ROWS = 512
PART_ROWS = 256

SUPER = 128
DN_GROUP = 8
assert SUPER == V7X_LANES and WINDOW == V7X_LANES and DN_HEAD_DIM == V7X_LANES
MASKED = -1e30

_EYE, _TRI0, _SREL0, _M8_0 = 0, 1, 3, 5
_MERGE_LEVELS = (3, 4, 5, 6)
_MERGE0 = 7
_NMASK = _MERGE0 + 2 * len(_MERGE_LEVELS)
_PREPARE_STAGES_BEFORE_SLOT_WRITE = 13


def _rms(x, g):
    return x * lax.rsqrt(jnp.mean(x * x, axis=-1, keepdims=True) + NORM_EPS) * g


def _dot(a, b):
    return jnp.dot(a, b, preferred_element_type=F32)


def _dot_nt(a, b):
    return lax.dot_general(a, b, (((1,), (1,)), ((), ())), preferred_element_type=F32)


def _dot_tn(a, b):
    return lax.dot_general(a, b, (((0,), (0,)), ((), ())), preferred_element_type=F32)


def _softplus(x):
    return jnp.maximum(x, 0.0) + jnp.log1p(jnp.exp(-jnp.abs(x)))


def _resident(shape):
    nd = len(shape)
    return pl.BlockSpec(shape, lambda *_: (0,) * nd, pipeline_mode=pl.Buffered(1))


def _params(sem):
    return pltpu.CompilerParams(dimension_semantics=sem, vmem_limit_bytes=V7X_VMEM_LIMIT_BYTES)


def _row_parts(tm):
    rows = PART_ROWS if tm % PART_ROWS == 0 else tm
    return [pl.ds(r0, rows) for r0 in range(0, tm, rows)]


def _ffn_body(x_ref, gpre_ref, gpost_ref, win_ref, wout_ref, o_ref, *, d_ff):
    parts = _row_parts(x_ref.shape[0])
    h = [_rms(x_ref[r, :], gpre_ref[...]).astype(BF16) for r in parts]
    gu = [_dot(hp, win_ref[...]) for hp in h]
    act = [(g[:, :d_ff] * jax.nn.sigmoid(g[:, :d_ff]) * g[:, d_ff:]).astype(BF16) for g in gu]
    y = [_dot(a, wout_ref[...]) for a in act]
    for r, yp in zip(parts, y):
        o_ref[r, :] = x_ref[r, :] + 0.5 * _rms(yp, gpost_ref[...])


def _ffn(x, gpre, gpost, w_in, w_out, tm):
    B, T, D = x.shape
    d_ff = w_out.shape[0]
    x2 = x.reshape(B * T, D)
    out = pl.pallas_call(
        functools.partial(_ffn_body, d_ff=d_ff),
        grid=(B * T // tm,),
        in_specs=[
            pl.BlockSpec((tm, D), lambda i: (i, 0)),
            _resident((1, D)),
            _resident((1, D)),
            _resident((D, 2 * d_ff)),
            _resident((d_ff, D)),
        ],
        out_specs=pl.BlockSpec((tm, D), lambda i: (i, 0)),
        out_shape=jax.ShapeDtypeStruct((B * T, D), F32),
        compiler_params=_params(("arbitrary",)),
        name="ffn",
    )(x2, gpre, gpost, w_in, w_out)
    return out.reshape(B, T, D)


def _proj_body(x_ref, g_ref, wdn_ref, wsr_ref, wat_ref, wg_ref, odn_ref, osr_ref, oat_ref, og_ref):
    for r in _row_parts(x_ref.shape[0]):
        u = _rms(x_ref[r, :], g_ref[...]).astype(BF16)
        odn_ref[r, :] = _dot(u, wdn_ref[...]).astype(odn_ref.dtype)
        osr_ref[:, r] = _dot_nt(wsr_ref[...], u)
        oat_ref[r, :] = _dot(u, wat_ref[...]).astype(oat_ref.dtype)
        og_ref[r, :] = _dot(u, wg_ref[...]).astype(og_ref.dtype)


def _proj(x, g, wdn, wsr, wat, wg, tm):
    B, T, D = x.shape
    tok = lambda n: pl.BlockSpec((None, tm, n), lambda b, i: (b, i, 0))
    nsr = wsr.shape[0]
    return pl.pallas_call(
        _proj_body,
        grid=(B, T // tm),
        in_specs=[tok(D), _resident((1, D)), _resident(wdn.shape),
                  _resident(wsr.shape), _resident(wat.shape), _resident(wg.shape)],
        out_specs=[tok(wdn.shape[1]),
                   pl.BlockSpec((None, nsr, tm), lambda b, i: (b, 0, i)),
                   tok(wat.shape[1]), tok(wg.shape[1])],
        out_shape=[
            jax.ShapeDtypeStruct((B, T, wdn.shape[1]), BF16),
            jax.ShapeDtypeStruct((B, nsr, T), F32),
            jax.ShapeDtypeStruct((B, T, wat.shape[1]), BF16),
            jax.ShapeDtypeStruct((B, T, wg.shape[1]), BF16),
        ],
        compiler_params=_params(("arbitrary", "arbitrary")),
        name="proj",
    )(x, g, wdn, wsr, wat, wg)


def _build_masks(msk):
    ii = lax.broadcasted_iota(jnp.int32, (SUPER, SUPER), 0)
    jj = lax.broadcasted_iota(jnp.int32, (SUPER, SUPER), 1)
    same = lambda s: jnp.right_shift(ii, s) == jnp.right_shift(jj, s)
    f = lambda c: jnp.where(c, 1.0, 0.0).astype(F32)
    msk[_EYE] = f(ii == jj)
    for d in range(2):
        after = (ii > jj) if d == 0 else (ii < jj)
        after_eq = (ii >= jj) if d == 0 else (ii <= jj)
        msk[_TRI0 + d] = f(after_eq)
        msk[_SREL0 + d] = f(after)
        msk[_M8_0 + d] = f(same(_MERGE_LEVELS[0]) & after)
        for n, lvl in enumerate(_MERGE_LEVELS):
            msk[_MERGE0 + 2 * n + d] = f(same(lvl + 1) & jnp.logical_not(same(lvl)) & after)


def _prepare_stages(chains, qn, kn, vn, gcol, grow, msk, bufs):
    mm = lambda a, b: _dot(a.astype(BF16), b.astype(BF16))
    eye = msk[_EYE]
    st = []
    for d, r0, slot in chains:
        rows = pl.ds(r0, SUPER)
        c = dict(d=d, slot=slot, rows=rows)
        c["kb"] = kn[rows, :].astype(BF16)
        c["vb"] = vn[rows, :].astype(BF16)
        grv = grow[:, rows]
        c["gc_r"] = grv[2 + d:3 + d, :]
        c["beta_r"] = grv[d:d + 1, :]
        c["qg"] = _dot_nt(jnp.concatenate([qn[rows, :].astype(BF16), c["kb"]], axis=0), c["kb"])
        st.append(c)
    yield
    for c in st:
        srel = msk[_SREL0 + c["d"]]
        dec = jnp.exp((gcol[2 + c["d"], c["rows"], :] - c["gc_r"]) * srel) * srel
        c["lm"] = c["qg"][SUPER:] * gcol[c["d"], c["rows"], :] * dec
        c["intra"] = (c["qg"][:SUPER] * (dec + eye)).astype(BF16)
        c["x"] = c["lm"] * msk[_M8_0 + c["d"]]
        c["x2"] = mm(c["x"], c["x"])
    yield
    for c in st:
        x34 = mm(c["x2"], jnp.concatenate([c["x"], c["x2"]], axis=1))
        c["x3"] = x34[:, :SUPER]
        c["x4"] = x34[:, SUPER:]
    yield
    for c in st:
        p1 = eye - c["x"] + c["x2"] - c["x3"]
        c["inv"] = p1 + mm(p1, c["x4"])
    yield
    for n in range(len(_MERGE_LEVELS)):
        for c in st:
            c["t"] = mm(c["inv"], c["lm"] * msk[_MERGE0 + 2 * n + c["d"]])
        yield
        for c in st:
            c["inv"] = c["inv"] - mm(c["t"], c["inv"])
        yield
    for c in st:
        tb = c["inv"] * c["beta_r"]
        tbe = tb * jnp.exp(c["gc_r"])
        u = _dot(tb.astype(BF16), c["vb"])
        w = _dot(tbe.astype(BF16), c["kb"])
        c["wu"] = jnp.concatenate([w, u], axis=1).astype(BF16)
    yield
    for c in st:
        ap, nn, o0b, et = bufs[c["d"]]
        iw = _dot(c["intra"], c["wu"])
        gc_c = gcol[2 + c["d"], c["rows"], :]
        c["qp"] = (qn[c["rows"], :] * jnp.exp(gc_c) - iw[:, :DN_HEAD_DIM]).astype(BF16)
        o0b[c["slot"]] = iw[:, DN_HEAD_DIM:]
    yield
    for c in st:
        ap, nn, o0b, et = bufs[c["d"]]
        last = SUPER - 1 if c["d"] == 0 else 0
        gc_c = gcol[2 + c["d"], c["rows"], :]
        tot = gc_c[last:last + 1, :]
        kdec = (kn[c["rows"], :] * jnp.exp(tot - gc_c)).astype(BF16)
        mn = _dot_tn(kdec, c["wu"])
        ap[c["slot"]] = jnp.concatenate([mn[:, :DN_HEAD_DIM].astype(BF16), c["qp"]], axis=0)
        nn[c["slot"]] = mn[:, DN_HEAD_DIM:]
        et[c["slot"]] = jnp.broadcast_to(jnp.exp(tot), (V7X_SUBLANES, DN_HEAD_DIM))
    yield


def _scan_stages(rows_f, rows_b, bufs, s_f, s_b, oacc):
    state = [s_f[...], s_b[...]]
    for slot, r0s in enumerate(zip(rows_f, rows_b)):
        for d in range(2):
            ap, nn, o0b, et = bufs[d]
            s = state[d]
            r = _dot(ap[slot], s.astype(BF16))
            oacc[pl.ds(r0s[d], SUPER), :] += r[DN_HEAD_DIM:] + o0b[slot]
            s3 = s.reshape(DN_HEAD_DIM // V7X_SUBLANES, V7X_SUBLANES, DN_HEAD_DIM) * et[slot][None]
            state[d] = s3.reshape(DN_HEAD_DIM, DN_HEAD_DIM) - r[:DN_HEAD_DIM] + nn[slot]
        yield
    s_f[...] = state[0]
    s_b[...] = state[1]
    yield


def _interleave(*stage_generators):
    live = list(stage_generators)
    while live:
        for gen in list(live):
            try:
                next(gen)
            except StopIteration:
                live.remove(gen)


def _dn_body(q_ref, k_ref, v_ref, z_ref, gr_ref, cwq_ref, cwk_ref, cwv_ref, lr_ref, nw_ref, o_ref,
             pad, qn, kn, vn, gcol, grow, oacc, s_f, s_b, msk,
             ap_f, nn_f, o0_f, et_f, ap_b, nn_b, o0_b, et_b, *, T, rt, group):
    nt = T // rt
    ns = T // SUPER
    halo = V7X_SUBLANES
    nsub = V7X_SUBLANES
    _build_masks(msk)

    xr = gr_ref[...]
    lr = lr_ref[...]
    rowi = lax.broadcasted_iota(jnp.int32, xr.shape, 0)
    grow[...] = jnp.where(rowi < 2, jax.nn.sigmoid(xr),
                          -jnp.exp(lr[:, 0:1]) * _softplus(xr + lr[:, 1:2]))

    ngate = 4
    tri_bf = jnp.concatenate([msk[_TRI0 + 1], msk[_TRI0]], axis=1).astype(BF16)
    row_s = lax.broadcasted_iota(jnp.int32, (nsub, SUPER), 0)
    sel_r = lax.broadcasted_iota(jnp.int32, (4 * nsub, ngate * V7X_LANES), 0)
    sel_n = lax.broadcasted_iota(jnp.int32, (4 * nsub, ngate * V7X_LANES), 1)
    sel = jnp.where(((sel_r & (nsub - 1)) == jnp.right_shift(sel_n, 7)) & (sel_r < 3 * nsub), 1.0, 0.0).astype(BF16)

    def split3(x):
        hi = x.astype(BF16).astype(F32)
        r1 = x - hi
        mid = r1.astype(BF16).astype(F32)
        lo = r1 - mid
        return jnp.concatenate([hi, mid, lo, jnp.zeros_like(x)], axis=0).astype(BF16)

    cum_unroll = _largest_divisor(ns, 8)

    def cum_tile(i, _):
        tiles = []
        for uu in range(cum_unroll):
            rows = pl.ds(pl.multiple_of((i * cum_unroll + uu) * SUPER, SUPER), SUPER)
            grv = grow[:, rows]
            tiles.append((rows, grv, _dot(split3(grv), tri_bf)))
        cols = []
        for rows, grv, rs in tiles:
            rs = rs[:nsub] + rs[nsub:2 * nsub] + rs[2 * nsub:3 * nsub]
            gnew = jnp.where(row_s == 2, rs[:, :SUPER], jnp.where(row_s == 3, rs[:, SUPER:], grv))
            grow[:, rows] = gnew
            cols.append((rows, _dot_tn(split3(gnew), sel)))
        for rows, col in cols:
            for j in range(ngate):
                gcol[j, rows, :] = col[:, j * V7X_LANES:(j + 1) * V7X_LANES]
        return 0

    lax.fori_loop(0, ns // cum_unroll, cum_tile, 0)

    streams = ((q_ref, cwq_ref[...], qn, DN_HEAD_DIM ** -0.5), (k_ref, cwk_ref[...], kn, 1.0),
               (v_ref, cwv_ref[...], vn, None))
    for n in range(len(streams)):
        pad[n, 0:halo, :] = jnp.zeros((halo, DN_HEAD_DIM), F32)
        pad[n, T + halo:T + 2 * halo, :] = jnp.zeros((halo, DN_HEAD_DIM), F32)

    def fill(i, _):
        r = pl.multiple_of(i * rt, rt)
        for n, (src, _, _, _) in enumerate(streams):
            pad[n, pl.ds(r + halo, rt), :] = src[pl.ds(r, rt), :].astype(F32)
        oacc[pl.ds(r, rt), :] = jnp.zeros((rt, DN_HEAD_DIM), F32)
        return 0

    lax.fori_loop(0, nt, fill, 0, unroll=2)

    def conv_rows(r0):
        base = r0 + halo - CONV_WIDTH // 2
        for n, (_, cw, dst, scale) in enumerate(streams):
            acc = pad[n, pl.ds(base, rt), :] * cw[0:1, :]
            for j in range(1, CONV_WIDTH):
                acc = acc + pad[n, pl.ds(base + j, rt), :] * cw[j:j + 1, :]
            y = acc * jax.nn.sigmoid(acc)
            if scale is not None:
                y = y * (lax.rsqrt(jnp.sum(y * y, axis=-1, keepdims=True) + NORM_EPS) * scale)
            dst[pl.ds(r0, rt), :] = y

    nw = nw_ref[...]

    def out_rows(r0):
        r = pl.ds(r0, rt)
        z = z_ref[r, :].astype(F32)
        o_ref[r, :] = (_rms(oacc[r, :], nw) * (z * jax.nn.sigmoid(z))).astype(o_ref.dtype)

    def loop_rows(fn, lo, hi):
        def tile(i, _):
            fn(pl.multiple_of(i * rt, rt))
            return 0
        lax.fori_loop(lo // rt, hi // rt, tile, 0, unroll=2)

    def row_stages(fn, lo, hi):
        for r0 in range(lo, hi, rt):
            fn(r0)
            yield

    s_f[...] = jnp.zeros((DN_HEAD_DIM, DN_HEAD_DIM), F32)
    s_b[...] = jnp.zeros((DN_HEAD_DIM, DN_HEAD_DIM), F32)
    fwd_row = lambda g, uu: pl.multiple_of((g * group + uu) * SUPER, SUPER)
    bwd_row = lambda g, uu: pl.multiple_of((ns - 1 - (g * group + uu)) * SUPER, SUPER)

    bufs = ((ap_f, nn_f, o0_f, et_f), (ap_b, nn_b, o0_b, et_b))

    def prepare(g):
        chains = [(d, (fwd_row, bwd_row)[d](g, uu), uu) for uu in range(group) for d in range(2)]
        return _prepare_stages(chains, qn, kn, vn, gcol, grow, msk, bufs)

    def scan(g):
        return _scan_stages([fwd_row(g, uu) for uu in range(group)], [bwd_row(g, uu) for uu in range(group)],
                            bufs, s_f, s_b, oacc)

    n_groups = ns // group
    edge = group * SUPER
    assert edge % rt == 0
    loop_rows(conv_rows, 0, edge)
    if n_groups >= 2:
        loop_rows(conv_rows, T - edge, T)
    _interleave(prepare(0), row_stages(conv_rows, edge, T - edge))

    def group_step(g, _):
        _interleave(prepare(g + 1), scan(g))
        return 0

    lax.fori_loop(0, n_groups - 1, group_step, 0)
    _interleave(scan(n_groups - 1), row_stages(out_rows, edge, T - edge))
    loop_rows(out_rows, 0, edge)
    if n_groups >= 2:
        loop_rows(out_rows, T - edge, T)


def _largest_divisor(n, cap):
    return max(d for d in range(1, cap + 1) if n % d == 0)


def _deltanet(y_dn, y_sr, conv_w, lr, nw):
    B, T, _ = y_dn.shape
    H, hd = DN_HEADS, DN_HEAD_DIM
    rt = min(256, T)
    ns = T // SUPER
    group = _largest_divisor(ns, DN_GROUP)
    assert group + 1 <= _PREPARE_STAGES_BEFORE_SLOT_WRITE
    col = lambda off: pl.BlockSpec((None, T, hd), lambda b, h: (b, 0, off + h))
    cw = lambda off: pl.BlockSpec((CONV_WIDTH, hd), lambda b, h: (0, off + h))
    tok = pltpu.VMEM((T, hd), F32)
    slots = [pltpu.VMEM((group, hd + SUPER, hd), BF16), pltpu.VMEM((group, hd, hd), F32),
             pltpu.VMEM((group, SUPER, hd), F32), pltpu.VMEM((group, V7X_SUBLANES, hd), F32)]
    return pl.pallas_call(
        functools.partial(_dn_body, T=T, rt=rt, group=group),
        grid=(B, H),
        in_specs=[col(0), col(H), col(2 * H), col(3 * H),
                  pl.BlockSpec((None, V7X_SUBLANES, T), lambda b, h: (b, h, 0)),
                  cw(0), cw(H), cw(2 * H),
                  pl.BlockSpec((None, V7X_SUBLANES, 2), lambda b, h: (h, 0, 0)),
                  pl.BlockSpec((1, hd), lambda b, h: (0, 0))],
        out_specs=pl.BlockSpec((None, T, hd), lambda b, h: (b, 0, h)),
        out_shape=jax.ShapeDtypeStruct((B, T, DN_WIDTH), BF16),
        scratch_shapes=[pltpu.VMEM((3, T + 2 * V7X_SUBLANES, hd), F32), tok, tok, tok,
                        pltpu.VMEM((4, T, V7X_LANES), F32), pltpu.VMEM((V7X_SUBLANES, T), F32), tok,
                        pltpu.VMEM((hd, hd), F32), pltpu.VMEM((hd, hd), F32),
                        pltpu.VMEM((_NMASK, SUPER, SUPER), F32)] + slots + slots,
        compiler_params=_params(("arbitrary", "arbitrary")),
        name="deltanet",
    )(y_dn, y_dn, y_dn, y_dn, y_sr, conv_w, conv_w, conv_w, lr, nw)


def _attn_body(sink_ref, q_ref, k_ref, v_ref, o_ref, k2, v2, bias, *, T, tq, rt):
    i = pl.program_id(1)
    nb = T // WINDOW
    half = ATTN_HEAD_DIM
    lane = lax.broadcasted_iota(jnp.int32, (WINDOW, V7X_LANES), 1)
    scale = ATTN_HEAD_DIM ** -0.5
    assert scale == 2.0 ** -3
    q_scale = (jnp.where(lane < half, scale, 0.0).astype(BF16), jnp.where(lane < half, 0.0, scale).astype(BF16))

    @pl.when(i == 0)
    def _init():
        zeros = jnp.zeros((WINDOW, V7X_LANES), BF16)
        for kv in range(ATTN_KV_HEADS):
            for r0 in (0, T + WINDOW):
                k2[kv, r0:r0 + WINDOW, :] = zeros
                v2[kv, r0:r0 + WINDOW, 0:V7X_LANES] = zeros
                v2[kv, r0:r0 + WINDOW, V7X_LANES:2 * V7X_LANES] = jnp.ones((WINDOW, V7X_LANES), BF16)
        lane_r = lax.broadcasted_iota(jnp.int32, (rt, V7X_LANES), 1)

        def fill(t, _):
            src = pl.ds(pl.multiple_of(t * rt, rt), rt)
            dst = pl.ds(pl.multiple_of(t * rt, rt) + WINDOW, rt)
            kf = k_ref[src, :].astype(F32)
            vf = v_ref[src, :].astype(F32)
            for kv in range(ATTN_KV_HEADS):
                sel = (lane_r < half) if kv == 0 else (lane_r >= half)
                km = jnp.where(sel, kf, 0.0)
                vm = jnp.where(sel, vf, 0.0)
                k2[kv, dst, :] = (km + pltpu.roll(km, half, 1)).astype(BF16)
                v2[kv, dst, 0:V7X_LANES] = (vm + pltpu.roll(vm, half, 1)).astype(BF16)
                v2[kv, dst, V7X_LANES:2 * V7X_LANES] = jnp.ones((rt, V7X_LANES), BF16)
            return 0

        lax.fori_loop(0, T // rt, fill, 0)
        a = lax.broadcasted_iota(jnp.int32, (WINDOW, 3 * WINDOW), 0)
        j = lax.broadcasted_iota(jnp.int32, (WINDOW, 3 * WINDOW), 1)
        dist = jnp.abs(j - WINDOW - a)
        for h in range(ATTN_Q_HEADS):
            slope = 2.0 ** (-8.0 * (h + 1) / ATTN_Q_HEADS)
            bias[h] = jnp.where(dist <= WINDOW, -slope * dist.astype(F32), MASKED)

    jrow = lax.broadcasted_iota(jnp.int32, (1, 3 * WINDOW), 1)

    def window(qi):
        return pl.ds(pl.multiple_of((i * (tq // WINDOW) + qi) * WINDOW, WINDOW), 3 * WINDOW)

    def score(qi, h):
        c0 = (h // 2) * V7X_LANES
        qm = q_ref[qi * WINDOW:(qi + 1) * WINDOW, c0:c0 + V7X_LANES] * q_scale[h % 2]
        return _dot_nt(qm, k2[h // ATTN_GROUP, window(qi), :])

    def attend(qi, h, s, pos_bias):
        sink = sink_ref[h]
        s = s + bias[h] + pos_bias
        m = jnp.maximum(jnp.max(s, axis=-1, keepdims=True), sink)
        p = jnp.exp(s - m).astype(BF16)
        ov = _dot(p, v2[h // ATTN_GROUP, window(qi), :])
        return ov[:, :V7X_LANES] / (ov[:, V7X_LANES:] + jnp.exp(sink - m))

    nblk = tq // WINDOW
    scores = [score(0, h) for h in range(ATTN_Q_HEADS)]
    for qi in range(nblk):
        gi = i * nblk + qi
        first_valid = jnp.where(gi == 0, WINDOW, 0)
        end_valid = jnp.where(gi == nb - 1, 2 * WINDOW, 3 * WINDOW)
        outside = (jrow < first_valid) | (jrow >= end_valid)
        pos_bias = jnp.where(outside, MASKED, 0.0)
        tiles = []
        for h in range(ATTN_Q_HEADS):
            tiles.append(attend(qi, h, scores[h], pos_bias))
            if qi + 1 < nblk:
                scores[h] = score(qi + 1, h)
        for pair in range(ATTN_Q_HEADS // 2):
            c0 = pair * V7X_LANES
            o_ref[qi * WINDOW:(qi + 1) * WINDOW, c0:c0 + V7X_LANES] = \
                jnp.where(lane < half, tiles[2 * pair], tiles[2 * pair + 1]).astype(o_ref.dtype)


def _attention(y_at, sink, tq):
    B, T, _ = y_at.shape
    rt = min(256, T)
    return pl.pallas_call(
        functools.partial(_attn_body, T=T, tq=tq, rt=rt),
        grid=(B, T // tq),
        in_specs=[pl.BlockSpec(memory_space=pltpu.SMEM),
                  pl.BlockSpec((None, tq, ATTN_WIDTH), lambda b, i: (b, i, 0)),
                  pl.BlockSpec((None, T, KV_WIDTH), lambda b, i: (b, 0, ATTN_WIDTH // KV_WIDTH)),
                  pl.BlockSpec((None, T, KV_WIDTH), lambda b, i: (b, 0, ATTN_WIDTH // KV_WIDTH + 1))],
        out_specs=pl.BlockSpec((None, tq, ATTN_WIDTH), lambda b, i: (b, i, 0)),
        out_shape=jax.ShapeDtypeStruct((B, T, ATTN_WIDTH), BF16),
        scratch_shapes=[pltpu.VMEM((ATTN_KV_HEADS, T + 2 * WINDOW, V7X_LANES), BF16),
                        pltpu.VMEM((ATTN_KV_HEADS, T + 2 * WINDOW, 2 * V7X_LANES), BF16),
                        pltpu.VMEM((ATTN_Q_HEADS, WINDOW, 3 * WINDOW), F32)],
        compiler_params=_params(("arbitrary", "arbitrary")),
        name="window_attn",
    )(sink, y_at, y_at, y_at)


def _merge_body(x_ref, a_ref, b_ref, g_ref, pa_ref, pb_ref, wo_ref, gpost_ref, o_ref, *, d_model):
    parts = _row_parts(x_ref.shape[0])
    ya = [_dot(a_ref[r, :], pa_ref[...]) for r in parts]
    yb = [_dot(b_ref[r, :], pb_ref[...]) for r in parts]
    mix = []
    for r, yap, ybp in zip(parts, ya, yb):
        g = g_ref[r, :].astype(F32)
        merged = jax.nn.sigmoid(g[:, :d_model]) * yap + jax.nn.sigmoid(g[:, d_model:]) * ybp
        mix.append(_dot(merged.astype(BF16), wo_ref[...]))
    for r, mp in zip(parts, mix):
        o_ref[r, :] = x_ref[r, :] + _rms(mp, gpost_ref[...])


def _merge(x, out_a, out_b, y_g, pa, pb, wo, gpost, tm):
    B, T, D = x.shape
    tok = lambda n: pl.BlockSpec((None, tm, n), lambda b, i: (b, i, 0))
    return pl.pallas_call(
        functools.partial(_merge_body, d_model=D),
        grid=(B, T // tm),
        in_specs=[tok(D), tok(out_a.shape[-1]), tok(out_b.shape[-1]), tok(y_g.shape[-1]),
                  _resident(pa.shape), _resident(pb.shape), _resident(wo.shape), _resident((1, D))],
        out_specs=tok(D),
        out_shape=jax.ShapeDtypeStruct((B, T, D), F32),
        compiler_params=_params(("arbitrary", "arbitrary")),
        name="merge",
    )(x, out_a, out_b, y_g, pa, pb, wo, gpost)


def _split_w_in(w_in):
    D = w_in.shape[0]
    H = DN_HEADS
    o_small = 4 * DN_WIDTH
    o_attn = o_small + 4 * H
    o_gate = o_attn + ATTN_WIDTH + 2 * KV_WIDTH
    wdn = w_in[:, :o_small].astype(BF16)
    small = w_in[:, o_small:o_attn].reshape(D, 4, H)
    small = jnp.transpose(small, (0, 2, 1))
    wsr =jnp.pad(small, ((0, 0), (0, 0), (0, V7X_SUBLANES - 4))).reshape(D, H * V7X_SUBLANES).T.astype(BF16)
    wat = w_in[:, o_attn:o_gate].astype(BF16)
    wg = w_in[:, o_gate:].astype(BF16)
    return wdn, wsr, wat, wg


def _gate_consts(a_log, dt_bias):
    H = DN_HEADS
    both = jnp.stack([a_log, dt_bias], axis=0).astype(F32)
    per_head = jnp.transpose(both, (2, 1, 0))
    return jnp.zeros((H, V7X_SUBLANES, 2), F32).at[:, 2:4, :].set(per_head)


def _tile(n, pref):
    t = min(pref, n)
    assert n % t == 0
    return t


def _trunk(x, layers):
    T = x.shape[1]
    tm = _tile(T, DENSE_TILE_ROWS)
    tq = _tile(T, ATTN_TILE_ROWS)
    for p in layers:
        x = _ffn(x, p["f1_pre"], p["f1_post"], p["f1_w_in"], p["f1_w_out"], tm)
        y_dn, y_sr, y_at, y_g = _proj(x, p["m_pre"], p["wdn"], p["wsr"], p["wat"], p["wg"], tm)
        out_a = _deltanet(y_dn, y_sr, p["conv_w"], p["lr"], p["dn_norm_w"])
        out_b = _attention(y_at, p["sink"], tq)
        x = _merge(x, out_a, out_b, y_g, p["pa"], p["pb"], p["wo"], p["m_post"], tm)
        x = _ffn(x, p["f2_pre"], p["f2_post"], p["f2_w_in"], p["f2_w_out"], tm)
    return x


def kernel(x_prompt, x_sample, ffn1_norm_pre, ffn1_norm_post, ffn1_w_in, ffn1_w_out, mix_norm_pre, mix_norm_post, mix_w_in, dn_conv_w, dn_a_log, dn_dt_bias, dn_norm_w, attn_sink, w_branch_a, w_branch_b, mix_w_out, ffn2_norm_pre, ffn2_norm_post, ffn2_w_in, ffn2_w_out):
    depth = ffn1_w_in.shape[0]
    row = lambda v: v.reshape(1, -1).astype(F32)
    layers = []
    for l in range(depth):
        wdn, wsr, wat, wg = _split_w_in(mix_w_in[l])
        lr = _gate_consts(dn_a_log[l], dn_dt_bias[l])
        layers.append(dict(
            f1_pre=row(ffn1_norm_pre[l]), f1_post=row(ffn1_norm_post[l]),
            f1_w_in=ffn1_w_in[l].astype(BF16), f1_w_out=ffn1_w_out[l].astype(BF16),
            m_pre=row(mix_norm_pre[l]), m_post=row(mix_norm_post[l]),
            wdn=wdn, wsr=wsr, wat=wat, wg=wg,
            conv_w=dn_conv_w[l].astype(F32), lr=lr, dn_norm_w=row(dn_norm_w[l]),
            sink=attn_sink[l].astype(F32),
            pa=w_branch_a[l].astype(BF16), pb=w_branch_b[l].astype(BF16), wo=mix_w_out[l].astype(BF16),
            f2_pre=row(ffn2_norm_pre[l]), f2_post=row(ffn2_norm_post[l]),
            f2_w_in=ffn2_w_in[l].astype(BF16), f2_w_out=ffn2_w_out[l].astype(BF16),
        ))
    return (_trunk(x_prompt, layers), _trunk(x_sample, layers))
```

```python
import functools

import jax
import jax.numpy as jnp
from jax import lax
from jax.experimental import pallas as pl
from jax.experimental.pallas import tpu as pltpu

F32 = jnp.float32
BF16 = jnp.bfloat16
HIGHEST = lax.Precision.HIGHEST

NORM_EPS = 1e-6
DN_HEADS = 4
DN_HEAD_DIM = 128
DN_WIDTH = DN_HEADS * DN_HEAD_DIM
CONV_WIDTH = 5
ATTN_Q_HEADS = 8
ATTN_KV_HEADS = 2
ATTN_GROUP = ATTN_Q_HEADS // ATTN_KV_HEADS
ATTN_HEAD_DIM = 64
ATTN_WIDTH = ATTN_Q_HEADS * ATTN_HEAD_DIM
KV_WIDTH = ATTN_KV_HEADS * ATTN_HEAD_DIM
WINDOW = 128

V7X_LANES = 128
V7X_SUBLANES = 8
V7X_VMEM_LIMIT_BYTES = 56 * 1024 * 1024
DENSE_TILE_ROWS = 1024
ATTN_TILE_ROWS = 512
PART_ROWS = 256

SUPER = 128
DN_GROUP = 8
assert SUPER == V7X_LANES and WINDOW == V7X_LANES and DN_HEAD_DIM == V7X_LANES
MASKED = -1e30

_EYE, _TRI0, _SREL0, _M8_0 = 0, 1, 3, 5
_MERGE_LEVELS = (3, 4, 5, 6)
_MERGE0 = 7
_NMASK = _MERGE0 + 2 * len(_MERGE_LEVELS)
_PREPARE_STAGES_BEFORE_SLOT_WRITE = 13


def _rms(x, g):
    return x * lax.rsqrt(jnp.mean(x * x, axis=-1, keepdims=True) + NORM_EPS) * g


def _dot(a, b):
    return jnp.dot(a, b, preferred_element_type=F32)


def _dot_nt(a, b):
    return lax.dot_general(a, b, (((1,), (1,)), ((), ())), preferred_element_type=F32)


def _dot_tn(a, b):
    return lax.dot_general(a, b, (((0,), (0,)), ((), ())), preferred_element_type=F32)


def _softplus(x):
    return jnp.maximum(x, 0.0) + jnp.log1p(jnp.exp(-jnp.abs(x)))


def _resident(shape):
    nd = len(shape)
    return pl.BlockSpec(shape, lambda *_: (0,) * nd, pipeline_mode=pl.Buffered(1))


def _params(sem):
    return pltpu.CompilerParams(dimension_semantics=sem, vmem_limit_bytes=V7X_VMEM_LIMIT_BYTES)


def _row_parts(tm):
    rows = PART_ROWS if tm % PART_ROWS == 0 else tm
    return [pl.ds(r0, rows) for r0 in range(0, tm, rows)]


def _ffn_body(x_ref, gpre_ref, gpost_ref, win_ref, wout_ref, o_ref, *, d_ff):
    parts = _row_parts(x_ref.shape[0])
    h = [_rms(x_ref[r, :], gpre_ref[...]).astype(BF16) for r in parts]
    gu = [_dot(hp, win_ref[...]) for hp in h]
    act = [(g[:, :d_ff] * jax.nn.sigmoid(g[:, :d_ff]) * g[:, d_ff:]).astype(BF16) for g in gu]
    y = [_dot(a, wout_ref[...]) for a in act]
    for r, yp in zip(parts, y):
        o_ref[r, :] = x_ref[r, :] + 0.5 * _rms(yp, gpost_ref[...])


def _ffn(x, gpre, gpost, w_in, w_out, tm):
    B, T, D = x.shape
    d_ff = w_out.shape[0]
    x2 = x.reshape(B * T, D)
    out = pl.pallas_call(
        functools.partial(_ffn_body, d_ff=d_ff),
        grid=(B * T // tm,),
        in_specs=[
            pl.BlockSpec((tm, D), lambda i: (i, 0)),
            _resident((1, D)),
            _resident((1, D)),
            _resident((D, 2 * d_ff)),
            _resident((d_ff, D)),
        ],
        out_specs=pl.BlockSpec((tm, D), lambda i: (i, 0)),
        out_shape=jax.ShapeDtypeStruct((B * T, D), F32),
        compiler_params=_params(("arbitrary",)),
        name="ffn",
    )(x2, gpre, gpost, w_in, w_out)
    return out.reshape(B, T, D)


def _proj_body(x_ref, g_ref, wdn_ref, wsr_ref, wat_ref, wg_ref, odn_ref, osr_ref, oat_ref, og_ref):
    for r in _row_parts(x_ref.shape[0]):
        u = _rms(x_ref[r, :], g_ref[...]).astype(BF16)
        odn_ref[r, :] = _dot(u, wdn_ref[...]).astype(odn_ref.dtype)
        osr_ref[:, r] = _dot_nt(wsr_ref[...], u)
        oat_ref[r, :] = _dot(u, wat_ref[...]).astype(oat_ref.dtype)
        og_ref[r, :] = _dot(u, wg_ref[...]).astype(og_ref.dtype)


def _proj(x, g, wdn, wsr, wat, wg, tm):
    B, T, D = x.shape
    tok = lambda n: pl.BlockSpec((None, tm, n), lambda b, i: (b, i, 0))
    nsr = wsr.shape[0]
    return pl.pallas_call(
        _proj_body,
        grid=(B, T // tm),
        in_specs=[tok(D), _resident((1, D)), _resident(wdn.shape),
                  _resident(wsr.shape), _resident(wat.shape), _resident(wg.shape)],
        out_specs=[tok(wdn.shape[1]),
                   pl.BlockSpec((None, nsr, tm), lambda b, i: (b, 0, i)),
                   tok(wat.shape[1]), tok(wg.shape[1])],
        out_shape=[
            jax.ShapeDtypeStruct((B, T, wdn.shape[1]), BF16),
            jax.ShapeDtypeStruct((B, nsr, T), F32),
            jax.ShapeDtypeStruct((B, T, wat.shape[1]), BF16),
            jax.ShapeDtypeStruct((B, T, wg.shape[1]), BF16),
        ],
        compiler_params=_params(("arbitrary", "arbitrary")),
        name="proj",
    )(x, g, wdn, wsr, wat, wg)


def _build_masks(msk):
    ii = lax.broadcasted_iota(jnp.int32, (SUPER, SUPER), 0)
    jj = lax.broadcasted_iota(jnp.int32, (SUPER, SUPER), 1)
    same = lambda s: jnp.right_shift(ii, s) == jnp.right_shift(jj, s)
    f = lambda c: jnp.where(c, 1.0, 0.0).astype(F32)
    msk[_EYE] = f(ii == jj)
    for d in range(2):
        after = (ii > jj) if d == 0 else (ii < jj)
        after_eq = (ii >= jj) if d == 0 else (ii <= jj)
        msk[_TRI0 + d] = f(after_eq)
        msk[_SREL0 + d] = f(after)
        msk[_M8_0 + d] = f(same(_MERGE_LEVELS[0]) & after)
        for n, lvl in enumerate(_MERGE_LEVELS):
            msk[_MERGE0 + 2 * n + d] = f(same(lvl + 1) & jnp.logical_not(same(lvl)) & after)


def _prepare_stages(chains, qn, kn, vn, gcol, grow, msk, bufs):
    mm = lambda a, b: _dot(a.astype(BF16), b.astype(BF16))
    eye = msk[_EYE]
    st = []
    for d, r0, slot in chains:
        rows = pl.ds(r0, SUPER)
        c = dict(d=d, slot=slot, rows=rows)
        c["kb"] = kn[rows, :].astype(BF16)
        c["vb"] = vn[rows, :].astype(BF16)
        grv = grow[:, rows]
        c["gc_r"] = grv[2 + d:3 + d, :]
        c["beta_r"] = grv[d:d + 1, :]
        c["qg"] = _dot_nt(jnp.concatenate([qn[rows, :].astype(BF16), c["kb"]], axis=0), c["kb"])
        st.append(c)
    yield
    for c in st:
        srel = msk[_SREL0 + c["d"]]
        dec = jnp.exp((gcol[2 + c["d"], c["rows"], :] - c["gc_r"]) * srel) * srel
        c["lm"] = c["qg"][SUPER:] * gcol[c["d"], c["rows"], :] * dec
        c["intra"] = (c["qg"][:SUPER] * (dec + eye)).astype(BF16)
        c["x"] = c["lm"] * msk[_M8_0 + c["d"]]
        c["x2"] = mm(c["x"], c["x"])
    yield
    for c in st:
        x34 = mm(c["x2"], jnp.concatenate([c["x"], c["x2"]], axis=1))
        c["x3"] = x34[:, :SUPER]
        c["x4"] = x34[:, SUPER:]
    yield
    for c in st:
        p1 = eye - c["x"] + c["x2"] - c["x3"]
        c["inv"] = p1 + mm(p1, c["x4"])
    yield
    for n, lvl in enumerate(_MERGE_LEVELS):
        blk = 1 << lvl
        later = lambda d, b: (b % 2 == 1) if d == 0 else (b % 2 == 0)
        pick = lambda a, d: jnp.concatenate(
            [a[b * blk:(b + 1) * blk] for b in range(SUPER // blk) if later(d, b)], axis=0)
        for c in st:
            c["t"] = mm(pick(c["inv"], c["d"]), c["lm"] * msk[_MERGE0 + 2 * n + c["d"]])
        yield
        for c in st:
            upd = pick(c["inv"], c["d"]) - mm(c["t"], c["inv"])
            pieces, taken = [], 0
            for b in range(SUPER // blk):
                if later(c["d"], b):
                    pieces.append(upd[taken:taken + blk])
                    taken += blk
                else:
                    pieces.append(c["inv"][b * blk:(b + 1) * blk])
            c["inv"] = jnp.concatenate(pieces, axis=0)
        yield
    for c in st:
        tb = c["inv"] * c["beta_r"]
        tbe = tb * jnp.exp(c["gc_r"])
        u = _dot(tb.astype(BF16), c["vb"])
        w = _dot(tbe.astype(BF16), c["kb"])
        c["wu"] = jnp.concatenate([w, u], axis=1).astype(BF16)
    yield
    for c in st:
        ap, nn, o0b, et = bufs[c["d"]]
        iw = _dot(c["intra"], c["wu"])
        gc_c = gcol[2 + c["d"], c["rows"], :]
        c["qp"] = (qn[c["rows"], :] * jnp.exp(gc_c) - iw[:, :DN_HEAD_DIM]).astype(BF16)
        o0b[c["slot"]] = iw[:, DN_HEAD_DIM:]
    yield
    for c in st:
        ap, nn, o0b, et = bufs[c["d"]]
        last = SUPER - 1 if c["d"] == 0 else 0
        gc_c = gcol[2 + c["d"], c["rows"], :]
        tot = gc_c[last:last + 1, :]
        kdec = (kn[c["rows"], :] * jnp.exp(tot - gc_c)).astype(BF16)
        mn = _dot_tn(kdec, c["wu"])
        ap[c["slot"]] = jnp.concatenate([mn[:, :DN_HEAD_DIM].astype(BF16), c["qp"]], axis=0)
        nn[c["slot"]] = mn[:, DN_HEAD_DIM:]
        et[c["slot"]] = jnp.broadcast_to(jnp.exp(tot), (V7X_SUBLANES, DN_HEAD_DIM))
    yield


def _scan_stages(rows_f, rows_b, bufs, s_f, s_b, oacc):
    state = [s_f[...], s_b[...]]
    for slot, r0s in enumerate(zip(rows_f, rows_b)):
        for d in range(2):
            ap, nn, o0b, et = bufs[d]
            s = state[d]
            r = _dot(ap[slot], s.astype(BF16))
            oacc[pl.ds(r0s[d], SUPER), :] += r[DN_HEAD_DIM:] + o0b[slot]
            s3 = s.reshape(DN_HEAD_DIM // V7X_SUBLANES, V7X_SUBLANES, DN_HEAD_DIM) * et[slot][None]
            state[d] = s3.reshape(DN_HEAD_DIM, DN_HEAD_DIM) - r[:DN_HEAD_DIM] + nn[slot]
        yield
    s_f[...] = state[0]
    s_b[...] = state[1]
    yield


def _interleave(*stage_generators):
    live = list(stage_generators)
    while live:
        for gen in list(live):
            try:
                next(gen)
            except StopIteration:
                live.remove(gen)


def _dn_body(q_ref, k_ref, v_ref, z_ref, gr_ref, cwq_ref, cwk_ref, cwv_ref, lr_ref, nw_ref, o_ref,
             pad, qn, kn, vn, gcol, grow, oacc, s_f, s_b, msk,
             ap_f, nn_f, o0_f, et_f, ap_b, nn_b, o0_b, et_b, *, T, rt, group):
    nt = T // rt
    ns = T // SUPER
    halo = V7X_SUBLANES
    nsub = V7X_SUBLANES
    _build_masks(msk)

    xr = gr_ref[...]
    lr = lr_ref[...]
    rowi = lax.broadcasted_iota(jnp.int32, xr.shape, 0)
    grow[...] = jnp.where(rowi < 2, jax.nn.sigmoid(xr),
                          -jnp.exp(lr[:, 0:1]) * _softplus(xr + lr[:, 1:2]))

    ngate = 4
    tri_bf = jnp.concatenate([msk[_TRI0 + 1], msk[_TRI0]], axis=1).astype(BF16)
    row_s = lax.broadcasted_iota(jnp.int32, (nsub, SUPER), 0)
    sel_r = lax.broadcasted_iota(jnp.int32, (4 * nsub, ngate * V7X_LANES), 0)
    sel_n = lax.broadcasted_iota(jnp.int32, (4 * nsub, ngate * V7X_LANES), 1)
    sel = jnp.where(((sel_r & (nsub - 1)) == jnp.right_shift(sel_n, 7)) & (sel_r < 3 * nsub), 1.0, 0.0).astype(BF16)

    def split3(x):
        hi = x.astype(BF16).astype(F32)
        r1 = x - hi
        mid = r1.astype(BF16).astype(F32)
        lo = r1 - mid
        return jnp.concatenate([hi, mid, lo, jnp.zeros_like(x)], axis=0).astype(BF16)

    cum_unroll = _largest_divisor(ns, 8)

    def cum_tile(i, _):
        tiles = []
        for uu in range(cum_unroll):
            rows = pl.ds(pl.multiple_of((i * cum_unroll + uu) * SUPER, SUPER), SUPER)
            grv = grow[:, rows]
            tiles.append((rows, grv, _dot(split3(grv), tri_bf)))
        cols = []
        for rows, grv, rs in tiles:
            rs = rs[:nsub] + rs[nsub:2 * nsub] + rs[2 * nsub:3 * nsub]
            gnew = jnp.where(row_s == 2, rs[:, :SUPER], jnp.where(row_s == 3, rs[:, SUPER:], grv))
            grow[:, rows] = gnew
            cols.append((rows, _dot_tn(split3(gnew), sel)))
        for rows, col in cols:
            for j in range(ngate):
                gcol[j, rows, :] = col[:, j * V7X_LANES:(j + 1) * V7X_LANES]
        return 0

    lax.fori_loop(0, ns // cum_unroll, cum_tile, 0)

    streams = ((q_ref, cwq_ref[...], qn, DN_HEAD_DIM ** -0.5), (k_ref, cwk_ref[...], kn, 1.0),
               (v_ref, cwv_ref[...], vn, None))
    for n in range(len(streams)):
        pad[n, 0:halo, :] = jnp.zeros((halo, DN_HEAD_DIM), F32)
        pad[n, T + halo:T + 2 * halo, :] = jnp.zeros((halo, DN_HEAD_DIM), F32)

    def fill(i, _):
        r = pl.multiple_of(i * rt, rt)
        for n, (src, _, _, _) in enumerate(streams):
            pad[n, pl.ds(r + halo, rt), :] = src[pl.ds(r, rt), :].astype(F32)
        oacc[pl.ds(r, rt), :] = jnp.zeros((rt, DN_HEAD_DIM), F32)
        return 0

    lax.fori_loop(0, nt, fill, 0, unroll=2)

    def conv_rows(r0):
        base = r0 + halo - CONV_WIDTH // 2
        for n, (_, cw, dst, scale) in enumerate(streams):
            acc = pad[n, pl.ds(base, rt), :] * cw[0:1, :]
            for j in range(1, CONV_WIDTH):
                acc = acc + pad[n, pl.ds(base + j, rt), :] * cw[j:j + 1, :]
            y = acc * jax.nn.sigmoid(acc)
            if scale is not None:
                y = y * (lax.rsqrt(jnp.sum(y * y, axis=-1, keepdims=True) + NORM_EPS) * scale)
            dst[pl.ds(r0, rt), :] = y

    nw = nw_ref[...]

    def out_rows(r0):
        r = pl.ds(r0, rt)
        z = z_ref[r, :].astype(F32)
        o_ref[r, :] = (_rms(oacc[r, :], nw) * (z * jax.nn.sigmoid(z))).astype(o_ref.dtype)

    def loop_rows(fn, lo, hi):
        def tile(i, _):
            fn(pl.multiple_of(i * rt, rt))
            return 0
        lax.fori_loop(lo // rt, hi // rt, tile, 0, unroll=2)

    def row_stages(fn, lo, hi):
        for r0 in range(lo, hi, rt):
            fn(r0)
            yield

    s_f[...] = jnp.zeros((DN_HEAD_DIM, DN_HEAD_DIM), F32)
    s_b[...] = jnp.zeros((DN_HEAD_DIM, DN_HEAD_DIM), F32)
    fwd_row = lambda g, uu: pl.multiple_of((g * group + uu) * SUPER, SUPER)
    bwd_row = lambda g, uu: pl.multiple_of((ns - 1 - (g * group + uu)) * SUPER, SUPER)

    bufs = ((ap_f, nn_f, o0_f, et_f), (ap_b, nn_b, o0_b, et_b))

    def prepare(g):
        chains = [(d, (fwd_row, bwd_row)[d](g, uu), uu) for uu in range(group) for d in range(2)]
        return _prepare_stages(chains, qn, kn, vn, gcol, grow, msk, bufs)

    def scan(g):
        return _scan_stages([fwd_row(g, uu) for uu in range(group)], [bwd_row(g, uu) for uu in range(group)],
                            bufs, s_f, s_b, oacc)

    n_groups = ns // group
    edge = group * SUPER
    assert edge % rt == 0
    loop_rows(conv_rows, 0, edge)
    if n_groups >= 2:
        loop_rows(conv_rows, T - edge, T)
    _interleave(prepare(0), row_stages(conv_rows, edge, T - edge))

    def group_step(g, _):
        _interleave(prepare(g + 1), scan(g))
        return 0

    lax.fori_loop(0, n_groups - 1, group_step, 0)
    _interleave(scan(n_groups - 1), row_stages(out_rows, edge, T - edge))
    loop_rows(out_rows, 0, edge)
    if n_groups >= 2:
        loop_rows(out_rows, T - edge, T)


def _largest_divisor(n, cap):
    return max(d for d in range(1, cap + 1) if n % d == 0)


def _deltanet(y_dn, y_sr, conv_w, lr, nw):
    B, T, _ = y_dn.shape
    H, hd = DN_HEADS, DN_HEAD_DIM
    rt = min(256, T)
    ns = T // SUPER
    group = _largest_divisor(ns, DN_GROUP)
    assert group + 1 <= _PREPARE_STAGES_BEFORE_SLOT_WRITE
    col = lambda off: pl.BlockSpec((None, T, hd), lambda b, h: (b, 0, off + h))
    cw = lambda off: pl.BlockSpec((CONV_WIDTH, hd), lambda b, h: (0, off + h))
    tok = pltpu.VMEM((T, hd), F32)
    slots = [pltpu.VMEM((group, hd + SUPER, hd), BF16), pltpu.VMEM((group, hd, hd), F32),
             pltpu.VMEM((group, SUPER, hd), F32), pltpu.VMEM((group, V7X_SUBLANES, hd), F32)]
    return pl.pallas_call(
        functools.partial(_dn_body, T=T, rt=rt, group=group),
        grid=(B, H),
        in_specs=[col(0), col(H), col(2 * H), col(3 * H),
                  pl.BlockSpec((None, V7X_SUBLANES, T), lambda b, h: (b, h, 0)),
                  cw(0), cw(H), cw(2 * H),
                  pl.BlockSpec((None, V7X_SUBLANES, 2), lambda b, h: (h, 0, 0)),
                  pl.BlockSpec((1, hd), lambda b, h: (0, 0))],
        out_specs=pl.BlockSpec((None, T, hd), lambda b, h: (b, 0, h)),
        out_shape=jax.ShapeDtypeStruct((B, T, DN_WIDTH), BF16),
        scratch_shapes=[pltpu.VMEM((3, T + 2 * V7X_SUBLANES, hd), F32), tok, tok, tok,
                        pltpu.VMEM((4, T, V7X_LANES), F32), pltpu.VMEM((V7X_SUBLANES, T), F32), tok,
                        pltpu.VMEM((hd, hd), F32), pltpu.VMEM((hd, hd), F32),
                        pltpu.VMEM((_NMASK, SUPER, SUPER), F32)] + slots + slots,
        compiler_params=_params(("arbitrary", "arbitrary")),
        name="deltanet",
    )(y_dn, y_dn, y_dn, y_dn, y_sr, conv_w, conv_w, conv_w, lr, nw)


def _attn_body(sink_ref, q_ref, k_ref, v_ref, o_ref, k2, v2, bias, *, T, tq, rt):
    i = pl.program_id(1)
    nb = T // WINDOW
    half = ATTN_HEAD_DIM
    lane = lax.broadcasted_iota(jnp.int32, (WINDOW, V7X_LANES), 1)
    scale = ATTN_HEAD_DIM ** -0.5
    assert scale == 2.0 ** -3
    q_scale = (jnp.where(lane < half, scale, 0.0).astype(BF16), jnp.where(lane < half, 0.0, scale).astype(BF16))

    @pl.when(i == 0)
    def _init():
        zeros = jnp.zeros((WINDOW, V7X_LANES), BF16)
        for kv in range(ATTN_KV_HEADS):
            for r0 in (0, T + WINDOW):
                k2[kv, r0:r0 + WINDOW, :] = zeros
                v2[kv, r0:r0 + WINDOW, 0:V7X_LANES] = zeros
                v2[kv, r0:r0 + WINDOW, V7X_LANES:2 * V7X_LANES] = jnp.ones((WINDOW, V7X_LANES), BF16)
        lane_r = lax.broadcasted_iota(jnp.int32, (rt, V7X_LANES), 1)

        def fill(t, _):
            src = pl.ds(pl.multiple_of(t * rt, rt), rt)
            dst = pl.ds(pl.multiple_of(t * rt, rt) + WINDOW, rt)
            kf = k_ref[src, :].astype(F32)
            vf = v_ref[src, :].astype(F32)
            for kv in range(ATTN_KV_HEADS):
                sel = (lane_r < half) if kv == 0 else (lane_r >= half)
                km = jnp.where(sel, kf, 0.0)
                vm = jnp.where(sel, vf, 0.0)
                k2[kv, dst, :] = (km + pltpu.roll(km, half, 1)).astype(BF16)
                v2[kv, dst, 0:V7X_LANES] = (vm + pltpu.roll(vm, half, 1)).astype(BF16)
                v2[kv, dst, V7X_LANES:2 * V7X_LANES] = jnp.ones((rt, V7X_LANES), BF16)
            return 0

        lax.fori_loop(0, T // rt, fill, 0)
        a = lax.broadcasted_iota(jnp.int32, (WINDOW, 3 * WINDOW), 0)
        j = lax.broadcasted_iota(jnp.int32, (WINDOW, 3 * WINDOW), 1)
        dist = jnp.abs(j - WINDOW - a)
        for h in range(ATTN_Q_HEADS):
            slope = 2.0 ** (-8.0 * (h + 1) / ATTN_Q_HEADS)
            bias[h] = jnp.where(dist <= WINDOW, -slope * dist.astype(F32), MASKED)

    jrow = lax.broadcasted_iota(jnp.int32, (1, 3 * WINDOW), 1)

    def window(qi):
        return pl.ds(pl.multiple_of((i * (tq // WINDOW) + qi) * WINDOW, WINDOW), 3 * WINDOW)

    def score(qi, h):
        c0 = (h // 2) * V7X_LANES
        qm = q_ref[qi * WINDOW:(qi + 1) * WINDOW, c0:c0 + V7X_LANES] * q_scale[h % 2]
        return _dot_nt(qm, k2[h // ATTN_GROUP, window(qi), :])

    def attend(qi, h, s, pos_bias):
        sink = sink_ref[h]
        s = s + bias[h] + pos_bias
        m = jnp.maximum(jnp.max(s, axis=-1, keepdims=True), sink)
        p = jnp.exp(s - m).astype(BF16)
        ov = _dot(p, v2[h // ATTN_GROUP, window(qi), :])
        return ov[:, :V7X_LANES] / (ov[:, V7X_LANES:] + jnp.exp(sink - m))

    nblk = tq // WINDOW
    scores = [score(0, h) for h in range(ATTN_Q_HEADS)]
    for qi in range(nblk):
        gi = i * nblk + qi
        first_valid = jnp.where(gi == 0, WINDOW, 0)
        end_valid = jnp.where(gi == nb - 1, 2 * WINDOW, 3 * WINDOW)
        outside = (jrow < first_valid) | (jrow >= end_valid)
        pos_bias = jnp.where(outside, MASKED, 0.0)
        tiles = []
        for h in range(ATTN_Q_HEADS):
            tiles.append(attend(qi, h, scores[h], pos_bias))
            if qi + 1 < nblk:
                scores[h] = score(qi + 1, h)
        for pair in range(ATTN_Q_HEADS // 2):
            c0 = pair * V7X_LANES
            o_ref[qi * WINDOW:(qi + 1) * WINDOW, c0:c0 + V7X_LANES] = \
                jnp.where(lane < half, tiles[2 * pair], tiles[2 * pair + 1]).astype(o_ref.dtype)


def _attention(y_at, sink, tq):
    B, T, _ = y_at.shape
    rt = min(256, T)
    return pl.pallas_call(
        functools.partial(_attn_body, T=T, tq=tq, rt=rt),
        grid=(B, T // tq),
        in_specs=[pl.BlockSpec(memory_space=pltpu.SMEM),
                  pl.BlockSpec((None, tq, ATTN_WIDTH), lambda b, i: (b, i, 0)),
                  pl.BlockSpec((None, T, KV_WIDTH), lambda b, i: (b, 0, ATTN_WIDTH // KV_WIDTH)),
                  pl.BlockSpec((None, T, KV_WIDTH), lambda b, i: (b, 0, ATTN_WIDTH // KV_WIDTH + 1))],
        out_specs=pl.BlockSpec((None, tq, ATTN_WIDTH), lambda b, i: (b, i, 0)),
        out_shape=jax.ShapeDtypeStruct((B, T, ATTN_WIDTH), BF16),
        scratch_shapes=[pltpu.VMEM((ATTN_KV_HEADS, T + 2 * WINDOW, V7X_LANES), BF16),
                        pltpu.VMEM((ATTN_KV_HEADS, T + 2 * WINDOW, 2 * V7X_LANES), BF16),
                        pltpu.VMEM((ATTN_Q_HEADS, WINDOW, 3 * WINDOW), F32)],
        compiler_params=_params(("arbitrary", "arbitrary")),
        name="window_attn",
    )(sink, y_at, y_at, y_at)


def _merge_body(x_ref, a_ref, b_ref, g_ref, pa_ref, pb_ref, wo_ref, gpost_ref, o_ref, *, d_model):
    parts = _row_parts(x_ref.shape[0])
    ya = [_dot(a_ref[r, :], pa_ref[...]) for r in parts]
    yb = [_dot(b_ref[r, :], pb_ref[...]) for r in parts]
    mix = []
    for r, yap, ybp in zip(parts, ya, yb):
        g = g_ref[r, :].astype(F32)
        merged = jax.nn.sigmoid(g[:, :d_model]) * yap + jax.nn.sigmoid(g[:, d_model:]) * ybp
        mix.append(_dot(merged.astype(BF16), wo_ref[...]))
    for r, mp in zip(parts, mix):
        o_ref[r, :] = x_ref[r, :] + _rms(mp, gpost_ref[...])


def _merge(x, out_a, out_b, y_g, pa, pb, wo, gpost, tm):
    B, T, D = x.shape
    tok = lambda n: pl.BlockSpec((None, tm, n), lambda b, i: (b, i, 0))
    return pl.pallas_call(
        functools.partial(_merge_body, d_model=D),
        grid=(B, T // tm),
        in_specs=[tok(D), tok(out_a.shape[-1]), tok(out_b.shape[-1]), tok(y_g.shape[-1]),
                  _resident(pa.shape), _resident(pb.shape), _resident(wo.shape), _resident((1, D))],
        out_specs=tok(D),
        out_shape=jax.ShapeDtypeStruct((B, T, D), F32),
        compiler_params=_params(("arbitrary", "arbitrary")),
        name="merge",
    )(x, out_a, out_b, y_g, pa, pb, wo, gpost)


def _split_w_in(w_in):
    D = w_in.shape[0]
    H = DN_HEADS
    o_small = 4 * DN_WIDTH
    o_attn = o_small + 4 * H
    o_gate = o_attn + ATTN_WIDTH + 2 * KV_WIDTH
    wdn = w_in[:, :o_small].astype(BF16)
    small = w_in[:, o_small:o_attn].reshape(D, 4, H)
    small = jnp.transpose(small, (0, 2, 1))
    wsr =jnp.pad(small, ((0, 0), (0, 0), (0, V7X_SUBLANES - 4))).reshape(D, H * V7X_SUBLANES).T.astype(BF16)
    wat = w_in[:, o_attn:o_gate].astype(BF16)
    wg = w_in[:, o_gate:].astype(BF16)
    return wdn, wsr, wat, wg


def _gate_consts(a_log, dt_bias):
    H = DN_HEADS
    both = jnp.stack([a_log, dt_bias], axis=0).astype(F32)
    per_head = jnp.transpose(both, (2, 1, 0))
    return jnp.zeros((H, V7X_SUBLANES, 2), F32).at[:, 2:4, :].set(per_head)


def _tile(n, pref):
    t = min(pref, n)
    assert n % t == 0
    return t


def _trunk(x, layers):
    T = x.shape[1]
    tm = _tile(T, DENSE_TILE_ROWS)
    tq = _tile(T, ATTN_TILE_ROWS)
    for p in layers:
        x = _ffn(x, p["f1_pre"], p["f1_post"], p["f1_w_in"], p["f1_w_out"], tm)
        y_dn, y_sr, y_at, y_g = _proj(x, p["m_pre"], p["wdn"], p["wsr"], p["wat"], p["wg"], tm)
        out_a = _deltanet(y_dn, y_sr, p["conv_w"], p["lr"], p["dn_norm_w"])
        out_b = _attention(y_at, p["sink"], tq)
        x = _merge(x, out_a, out_b, y_g, p["pa"], p["pb"], p["wo"], p["m_post"], tm)
        x = _ffn(x, p["f2_pre"], p["f2_post"], p["f2_w_in"], p["f2_w_out"], tm)
    return x


def kernel(x_prompt, x_sample, ffn1_norm_pre, ffn1_norm_post, ffn1_w_in, ffn1_w_out, mix_norm_pre, mix_norm_post, mix_w_in, dn_conv_w, dn_a_log, dn_dt_bias, dn_norm_w, attn_sink, w_branch_a, w_branch_b, mix_w_out, ffn2_norm_pre, ffn2_norm_post, ffn2_w_in, ffn2_w_out):
    depth = ffn1_w_in.shape[0]
    row = lambda v: v.reshape(1, -1).astype(F32)
    layers = []
    for l in range(depth):
        wdn, wsr, wat, wg = _split_w_in(mix_w_in[l])
        lr = _gate_consts(dn_a_log[l], dn_dt_bias[l])
        layers.append(dict(
            f1_pre=row(ffn1_norm_pre[l]), f1_post=row(ffn1_norm_post[l]),
            f1_w_in=ffn1_w_in[l].astype(BF16), f1_w_out=ffn1_w_out[l].astype(BF16),
            m_pre=row(mix_norm_pre[l]), m_post=row(mix_norm_post[l]),
            wdn=wdn, wsr=wsr, wat=wat, wg=wg,
            conv_w=dn_conv_w[l].astype(F32), lr=lr, dn_norm_w=row(dn_norm_w[l]),
            sink=attn_sink[l].astype(F32),
            pa=w_branch_a[l].astype(BF16), pb=w_branch_b[l].astype(BF16), wo=mix_w_out[l].astype(BF16),
            f2_pre=row(ffn2_norm_pre[l]), f2_post=row(ffn2_norm_post[l]),
            f2_w_in=ffn2_w_in[l].astype(BF16), f2_w_out=ffn2_w_out[l].astype(BF16),
        ))
    return (_trunk(x_prompt, layers), _trunk(x_sample, layers))
```

```python
import functools

import jax
import jax.numpy as jnp
from jax import lax
from jax.experimental import pallas as pl
from jax.experimental.pallas import tpu as pltpu

F32 = jnp.float32
BF16 = jnp.bfloat16
HIGHEST = lax.Precision.HIGHEST

NORM_EPS = 1e-6
DN_HEADS = 4
DN_HEAD_DIM = 128
DN_WIDTH = DN_HEADS * DN_HEAD_DIM
CONV_WIDTH = 5
ATTN_Q_HEADS = 8
ATTN_KV_HEADS = 2
ATTN_GROUP = ATTN_Q_HEADS // ATTN_KV_HEADS
ATTN_HEAD_DIM = 64
ATTN_WIDTH = ATTN_Q_HEADS * ATTN_HEAD_DIM
KV_WIDTH = ATTN_KV_HEADS * ATTN_HEAD_DIM
WINDOW = 128

V7X_LANES = 128
V7X_SUBLANES = 8
V7X_VMEM_LIMIT_BYTES = 56 * 1024 * 1024
DENSE_TILE_ROWS = 1024
ATTN_TILE_ROWS = 512
PART_ROWS = 256

SUPER = 128
DN_GROUP = 8
assert SUPER == V7X_LANES and WINDOW == V7X_LANES and DN_HEAD_DIM == V7X_LANES
MASKED = -1e30

_EYE, _TRI0, _SREL0, _M8_0 = 0, 1, 3, 5
_MERGE_LEVELS = (3, 4, 5, 6)
_MERGE0 = 7
_NMASK = _MERGE0 + 2 * len(_MERGE_LEVELS)
_PREPARE_STAGES_BEFORE_SLOT_WRITE = 13


def _rms(x, g):
    return x * lax.rsqrt(jnp.mean(x * x, axis=-1, keepdims=True) + NORM_EPS) * g


def _dot(a, b):
    return jnp.dot(a, b, preferred_element_type=F32)


def _dot_nt(a, b):
    return lax.dot_general(a, b, (((1,), (1,)), ((), ())), preferred_element_type=F32)


def _dot_tn(a, b):
    return lax.dot_general(a, b, (((0,), (0,)), ((), ())), preferred_element_type=F32)


def _softplus(x):
    return jnp.maximum(x, 0.0) + jnp.log1p(jnp.exp(-jnp.abs(x)))


def _resident(shape):
    nd = len(shape)
    return pl.BlockSpec(shape, lambda *_: (0,) * nd, pipeline_mode=pl.Buffered(1))


def _params(sem):
    return pltpu.CompilerParams(dimension_semantics=sem, vmem_limit_bytes=V7X_VMEM_LIMIT_BYTES)


def _row_parts(tm):
    rows = PART_ROWS if tm % PART_ROWS == 0 else tm
    return [pl.ds(r0, rows) for r0 in range(0, tm, rows)]


def _ffn_body(x_ref, gpre_ref, gpost_ref, win_ref, wout_ref, o_ref, *, d_ff):
    parts = _row_parts(x_ref.shape[0])
    h = [_rms(x_ref[r, :], gpre_ref[...]).astype(BF16) for r in parts]
    gu = [_dot(hp, win_ref[...]) for hp in h]
    act = [(g[:, :d_ff] * jax.nn.sigmoid(g[:, :d_ff]) * g[:, d_ff:]).astype(BF16) for g in gu]
    y = [_dot(a, wout_ref[...]) for a in act]
    for r, yp in zip(parts, y):
        o_ref[r, :] = x_ref[r, :] + 0.5 * _rms(yp, gpost_ref[...])


def _ffn(x, gpre, gpost, w_in, w_out, tm):
    B, T, D = x.shape
    d_ff = w_out.shape[0]
    x2 = x.reshape(B * T, D)
    out = pl.pallas_call(
        functools.partial(_ffn_body, d_ff=d_ff),
        grid=(B * T // tm,),
        in_specs=[
            pl.BlockSpec((tm, D), lambda i: (i, 0)),
            _resident((1, D)),
            _resident((1, D)),
            _resident((D, 2 * d_ff)),
            _resident((d_ff, D)),
        ],
        out_specs=pl.BlockSpec((tm, D), lambda i: (i, 0)),
        out_shape=jax.ShapeDtypeStruct((B * T, D), F32),
        compiler_params=_params(("arbitrary",)),
        name="ffn",
    )(x2, gpre, gpost, w_in, w_out)
    return out.reshape(B, T, D)


def _proj_body(x_ref, g_ref, wdn_ref, wsr_ref, wat_ref, wg_ref, odn_ref, osr_ref, oat_ref, og_ref):
    for r in _row_parts(x_ref.shape[0]):
        u = _rms(x_ref[r, :], g_ref[...]).astype(BF16)
        ydn = _dot(u, wdn_ref[...]).astype(odn_ref.dtype)
        for j in range(odn_ref.shape[0]):
            odn_ref[j, r, :] = ydn[:, j * V7X_LANES:(j + 1) * V7X_LANES]
        osr_ref[:, r] = _dot_nt(wsr_ref[...], u)
        oat_ref[r, :] = _dot(u, wat_ref[...]).astype(oat_ref.dtype)
        og_ref[r, :] = _dot(u, wg_ref[...]).astype(og_ref.dtype)


def _proj(x, g, wdn, wsr, wat, wg, tm):
    B, T, D = x.shape
    tok = lambda n: pl.BlockSpec((None, tm, n), lambda b, i: (b, i, 0))
    nsr = wsr.shape[0]
    return pl.pallas_call(
        _proj_body,
        grid=(B, T // tm),
        in_specs=[tok(D), _resident((1, D)), _resident(wdn.shape),
                  _resident(wsr.shape), _resident(wat.shape), _resident(wg.shape)],
        out_specs=[pl.BlockSpec((None, wdn.shape[1] // V7X_LANES, tm, V7X_LANES), lambda b, i: (b, 0, i, 0)),
                   pl.BlockSpec((None, nsr, tm), lambda b, i: (b, 0, i)),
                   tok(wat.shape[1]), tok(wg.shape[1])],
        out_shape=[
            jax.ShapeDtypeStruct((B, wdn.shape[1] // V7X_LANES, T, V7X_LANES), BF16),
            jax.ShapeDtypeStruct((B, nsr, T), F32),
            jax.ShapeDtypeStruct((B, T, wat.shape[1]), BF16),
            jax.ShapeDtypeStruct((B, T, wg.shape[1]), BF16),
        ],
        compiler_params=_params(("arbitrary", "arbitrary")),
        name="proj",
    )(x, g, wdn, wsr, wat, wg)


def _build_masks(msk):
    ii = lax.broadcasted_iota(jnp.int32, (SUPER, SUPER), 0)
    jj = lax.broadcasted_iota(jnp.int32, (SUPER, SUPER), 1)
    same = lambda s: jnp.right_shift(ii, s) == jnp.right_shift(jj, s)
    f = lambda c: jnp.where(c, 1.0, 0.0).astype(F32)
    msk[_EYE] = f(ii == jj)
    for d in range(2):
        after = (ii > jj) if d == 0 else (ii < jj)
        after_eq = (ii >= jj) if d == 0 else (ii <= jj)
        msk[_TRI0 + d] = f(after_eq)
        msk[_SREL0 + d] = f(after)
        msk[_M8_0 + d] = f(same(_MERGE_LEVELS[0]) & after)
        for n, lvl in enumerate(_MERGE_LEVELS):
            msk[_MERGE0 + 2 * n + d] = f(same(lvl + 1) & jnp.logical_not(same(lvl)) & after)


def _prepare_stages(chains, qn, kn, vn, gcol, grow, msk, bufs):
    mm = lambda a, b: _dot(a.astype(BF16), b.astype(BF16))
    eye = msk[_EYE]
    st = []
    for d, r0, slot in chains:
        rows = pl.ds(r0, SUPER)
        c = dict(d=d, slot=slot, rows=rows)
        c["kb"] = kn[rows, :].astype(BF16)
        c["vb"] = vn[rows, :].astype(BF16)
        grv = grow[:, rows]
        c["gc_r"] = grv[2 + d:3 + d, :]
        c["beta_r"] = grv[d:d + 1, :]
        c["qg"] = _dot_nt(jnp.concatenate([qn[rows, :].astype(BF16), c["kb"]], axis=0), c["kb"])
        st.append(c)
    yield
    for c in st:
        srel = msk[_SREL0 + c["d"]]
        dec = jnp.exp((gcol[2 + c["d"], c["rows"], :] - c["gc_r"]) * srel) * srel
        c["lm"] = c["qg"][SUPER:] * gcol[c["d"], c["rows"], :] * dec
        c["intra"] = (c["qg"][:SUPER] * (dec + eye)).astype(BF16)
        c["x"] = c["lm"] * msk[_M8_0 + c["d"]]
        c["x2"] = mm(c["x"], c["x"])
    yield
    for c in st:
        x34 = mm(c["x2"], jnp.concatenate([c["x"], c["x2"]], axis=1))
        c["x3"] = x34[:, :SUPER]
        c["x4"] = x34[:, SUPER:]
    yield
    for c in st:
        p1 = eye - c["x"] + c["x2"] - c["x3"]
        c["inv"] = p1 + mm(p1, c["x4"])
    yield
    for n, lvl in enumerate(_MERGE_LEVELS):
        blk = 1 << lvl
        later = lambda d, b: (b % 2 == 1) if d == 0 else (b % 2 == 0)
        pick = lambda a, d: jnp.concatenate(
            [a[b * blk:(b + 1) * blk] for b in range(SUPER // blk) if later(d, b)], axis=0)
        for c in st:
            c["t"] = mm(pick(c["inv"], c["d"]), c["lm"] * msk[_MERGE0 + 2 * n + c["d"]])
        yield
        for c in st:
            upd = pick(c["inv"], c["d"]) - mm(c["t"], c["inv"])
            pieces, taken = [], 0
            for b in range(SUPER // blk):
                if later(c["d"], b):
                    pieces.append(upd[taken:taken + blk])
                    taken += blk
                else:
                    pieces.append(c["inv"][b * blk:(b + 1) * blk])
            c["inv"] = jnp.concatenate(pieces, axis=0)
        yield
    for c in st:
        tb = c["inv"] * c["beta_r"]
        tbe = tb * jnp.exp(c["gc_r"])
        u = _dot(tb.astype(BF16), c["vb"])
        w = _dot(tbe.astype(BF16), c["kb"])
        c["wu"] = jnp.concatenate([w, u], axis=1).astype(BF16)
    yield
    for c in st:
        ap, nn, o0b, et = bufs[c["d"]]
        iw = _dot(c["intra"], c["wu"])
        gc_c = gcol[2 + c["d"], c["rows"], :]
        c["qp"] = (qn[c["rows"], :] * jnp.exp(gc_c) - iw[:, :DN_HEAD_DIM]).astype(BF16)
        o0b[c["slot"]] = iw[:, DN_HEAD_DIM:]
    yield
    for c in st:
        ap, nn, o0b, et = bufs[c["d"]]
        last = SUPER - 1 if c["d"] == 0 else 0
        gc_c = gcol[2 + c["d"], c["rows"], :]
        tot = gc_c[last:last + 1, :]
        kdec = (kn[c["rows"], :] * jnp.exp(tot - gc_c)).astype(BF16)
        mn = _dot_tn(kdec, c["wu"])
        ap[c["slot"]] = jnp.concatenate([mn[:, :DN_HEAD_DIM].astype(BF16), c["qp"]], axis=0)
        nn[c["slot"]] = mn[:, DN_HEAD_DIM:]
        et[c["slot"]] = jnp.broadcast_to(jnp.exp(tot), (V7X_SUBLANES, DN_HEAD_DIM))
    yield


def _scan_stages(rows_f, rows_b, bufs, s_f, s_b, oacc):
    state = [s_f[...], s_b[...]]
    for slot, r0s in enumerate(zip(rows_f, rows_b)):
        for d in range(2):
            ap, nn, o0b, et = bufs[d]
            s = state[d]
            r = _dot(ap[slot], s.astype(BF16))
            oacc[pl.ds(r0s[d], SUPER), :] += r[DN_HEAD_DIM:] + o0b[slot]
            s3 = s.reshape(DN_HEAD_DIM // V7X_SUBLANES, V7X_SUBLANES, DN_HEAD_DIM) * et[slot][None]
            state[d] = s3.reshape(DN_HEAD_DIM, DN_HEAD_DIM) - r[:DN_HEAD_DIM] + nn[slot]
        yield
    s_f[...] = state[0]
    s_b[...] = state[1]
    yield


def _interleave(*stage_generators):
    live = list(stage_generators)
    while live:
        for gen in list(live):
            try:
                next(gen)
            except StopIteration:
                live.remove(gen)


def _dn_body(q_ref, k_ref, v_ref, z_ref, gr_ref, cwq_ref, cwk_ref, cwv_ref, lr_ref, nw_ref, o_ref,
             pad, qn, kn, vn, gcol, grow, oacc, s_f, s_b, msk,
             ap_f, nn_f, o0_f, et_f, ap_b, nn_b, o0_b, et_b, *, T, rt, group):
    nt = T // rt
    ns = T // SUPER
    halo = V7X_SUBLANES
    nsub = V7X_SUBLANES
    _build_masks(msk)

    xr = gr_ref[...]
    lr = lr_ref[...]
    rowi = lax.broadcasted_iota(jnp.int32, xr.shape, 0)
    grow[...] = jnp.where(rowi < 2, jax.nn.sigmoid(xr),
                          -jnp.exp(lr[:, 0:1]) * _softplus(xr + lr[:, 1:2]))

    ngate = 4
    tri_bf = jnp.concatenate([msk[_TRI0 + 1], msk[_TRI0]], axis=1).astype(BF16)
    row_s = lax.broadcasted_iota(jnp.int32, (nsub, SUPER), 0)
    sel_r = lax.broadcasted_iota(jnp.int32, (4 * nsub, ngate * V7X_LANES), 0)
    sel_n = lax.broadcasted_iota(jnp.int32, (4 * nsub, ngate * V7X_LANES), 1)
    sel = jnp.where(((sel_r & (nsub - 1)) == jnp.right_shift(sel_n, 7)) & (sel_r < 3 * nsub), 1.0, 0.0).astype(BF16)

    def split3(x):
        hi = x.astype(BF16).astype(F32)
        r1 = x - hi
        mid = r1.astype(BF16).astype(F32)
        lo = r1 - mid
        return jnp.concatenate([hi, mid, lo, jnp.zeros_like(x)], axis=0).astype(BF16)

    cum_unroll = _largest_divisor(ns, 8)

    def cum_tile(i, _):
        tiles = []
        for uu in range(cum_unroll):
            rows = pl.ds(pl.multiple_of((i * cum_unroll + uu) * SUPER, SUPER), SUPER)
            grv = grow[:, rows]
            tiles.append((rows, grv, _dot(split3(grv), tri_bf)))
        cols = []
        for rows, grv, rs in tiles:
            rs = rs[:nsub] + rs[nsub:2 * nsub] + rs[2 * nsub:3 * nsub]
            gnew = jnp.where(row_s == 2, rs[:, :SUPER], jnp.where(row_s == 3, rs[:, SUPER:], grv))
            grow[:, rows] = gnew
            cols.append((rows, _dot_tn(split3(gnew), sel)))
        for rows, col in cols:
            for j in range(ngate):
                gcol[j, rows, :] = col[:, j * V7X_LANES:(j + 1) * V7X_LANES]
        return 0

    lax.fori_loop(0, ns // cum_unroll, cum_tile, 0)

    streams = ((q_ref, cwq_ref[...], qn, DN_HEAD_DIM ** -0.5), (k_ref, cwk_ref[...], kn, 1.0),
               (v_ref, cwv_ref[...], vn, None))
    for n in range(len(streams)):
        pad[n, 0:halo, :] = jnp.zeros((halo, DN_HEAD_DIM), F32)
        pad[n, T + halo:T + 2 * halo, :] = jnp.zeros((halo, DN_HEAD_DIM), F32)

    def fill(i, _):
        r = pl.multiple_of(i * rt, rt)
        for n, (src, _, _, _) in enumerate(streams):
            pad[n, pl.ds(r + halo, rt), :] = src[pl.ds(r, rt), :].astype(F32)
        oacc[pl.ds(r, rt), :] = jnp.zeros((rt, DN_HEAD_DIM), F32)
        return 0

    lax.fori_loop(0, nt, fill, 0, unroll=2)

    def conv_rows(r0):
        base = r0 + halo - CONV_WIDTH // 2
        for n, (_, cw, dst, scale) in enumerate(streams):
            acc = pad[n, pl.ds(base, rt), :] * cw[0:1, :]
            for j in range(1, CONV_WIDTH):
                acc = acc + pad[n, pl.ds(base + j, rt), :] * cw[j:j + 1, :]
            y = acc * jax.nn.sigmoid(acc)
            if scale is not None:
                y = y * (lax.rsqrt(jnp.sum(y * y, axis=-1, keepdims=True) + NORM_EPS) * scale)
            dst[pl.ds(r0, rt), :] = y

    nw = nw_ref[...]

    def out_rows(r0):
        r = pl.ds(r0, rt)
        z = z_ref[r, :].astype(F32)
        o_ref[r, :] = (_rms(oacc[r, :], nw) * (z * jax.nn.sigmoid(z))).astype(o_ref.dtype)

    def loop_rows(fn, lo, hi):
        def tile(i, _):
            fn(pl.multiple_of(i * rt, rt))
            return 0
        lax.fori_loop(lo // rt, hi // rt, tile, 0, unroll=2)

    def row_stages(fn, lo, hi):
        for r0 in range(lo, hi, rt):
            fn(r0)
            yield

    s_f[...] = jnp.zeros((DN_HEAD_DIM, DN_HEAD_DIM), F32)
    s_b[...] = jnp.zeros((DN_HEAD_DIM, DN_HEAD_DIM), F32)
    fwd_row = lambda g, uu: pl.multiple_of((g * group + uu) * SUPER, SUPER)
    bwd_row = lambda g, uu: pl.multiple_of((ns - 1 - (g * group + uu)) * SUPER, SUPER)

    bufs = ((ap_f, nn_f, o0_f, et_f), (ap_b, nn_b, o0_b, et_b))

    def prepare(g):
        chains = [(d, (fwd_row, bwd_row)[d](g, uu), uu) for uu in range(group) for d in range(2)]
        return _prepare_stages(chains, qn, kn, vn, gcol, grow, msk, bufs)

    def scan(g):
        return _scan_stages([fwd_row(g, uu) for uu in range(group)], [bwd_row(g, uu) for uu in range(group)],
                            bufs, s_f, s_b, oacc)

    n_groups = ns // group
    edge = group * SUPER
    assert edge % rt == 0
    loop_rows(conv_rows, 0, edge)
    if n_groups >= 2:
        loop_rows(conv_rows, T - edge, T)
    _interleave(prepare(0), row_stages(conv_rows, edge, T - edge))

    def group_step(g, _):
        _interleave(prepare(g + 1), scan(g))
        return 0

    lax.fori_loop(0, n_groups - 1, group_step, 0)
    _interleave(scan(n_groups - 1), row_stages(out_rows, edge, T - edge))
    loop_rows(out_rows, 0, edge)
    if n_groups >= 2:
        loop_rows(out_rows, T - edge, T)


def _largest_divisor(n, cap):
    return max(d for d in range(1, cap + 1) if n % d == 0)


def _deltanet(y_dn, y_sr, conv_w, lr, nw):
    B, _, T, _ = y_dn.shape
    H, hd = DN_HEADS, DN_HEAD_DIM
    rt = min(256, T)
    ns = T // SUPER
    group = _largest_divisor(ns, DN_GROUP)
    assert group + 1 <= _PREPARE_STAGES_BEFORE_SLOT_WRITE
    col = lambda off: pl.BlockSpec((None, None, T, hd), lambda b, h: (b, off + h, 0, 0))
    cw = lambda off: pl.BlockSpec((CONV_WIDTH, hd), lambda b, h: (0, off + h))
    tok = pltpu.VMEM((T, hd), F32)
    slots = [pltpu.VMEM((group, hd + SUPER, hd), BF16), pltpu.VMEM((group, hd, hd), F32),
             pltpu.VMEM((group, SUPER, hd), F32), pltpu.VMEM((group, V7X_SUBLANES, hd), F32)]
    return pl.pallas_call(
        functools.partial(_dn_body, T=T, rt=rt, group=group),
        grid=(B, H),
        in_specs=[col(0), col(H), col(2 * H), col(3 * H),
                  pl.BlockSpec((None, V7X_SUBLANES, T), lambda b, h: (b, h, 0)),
                  cw(0), cw(H), cw(2 * H),
                  pl.BlockSpec((None, V7X_SUBLANES, 2), lambda b, h: (h, 0, 0)),
                  pl.BlockSpec((1, hd), lambda b, h: (0, 0))],
        out_specs=pl.BlockSpec((None, None, T, hd), lambda b, h: (b, h, 0, 0)),
        out_shape=jax.ShapeDtypeStruct((B, H, T, hd), BF16),
        scratch_shapes=[pltpu.VMEM((3, T + 2 * V7X_SUBLANES, hd), F32), tok, tok, tok,
                        pltpu.VMEM((4, T, V7X_LANES), F32), pltpu.VMEM((V7X_SUBLANES, T), F32), tok,
                        pltpu.VMEM((hd, hd), F32), pltpu.VMEM((hd, hd), F32),
                        pltpu.VMEM((_NMASK, SUPER, SUPER), F32)] + slots + slots,
        compiler_params=_params(("arbitrary", "arbitrary")),
        name="deltanet",
    )(y_dn, y_dn, y_dn, y_dn, y_sr, conv_w, conv_w, conv_w, lr, nw)


def _attn_body(sink_ref, q_ref, k_ref, v_ref, o_ref, k2, v2, bias, *, T, tq, rt):
    i = pl.program_id(1)
    nb = T // WINDOW
    half = ATTN_HEAD_DIM
    lane = lax.broadcasted_iota(jnp.int32, (WINDOW, V7X_LANES), 1)
    scale = ATTN_HEAD_DIM ** -0.5
    assert scale == 2.0 ** -3
    q_scale = (jnp.where(lane < half, scale, 0.0).astype(BF16), jnp.where(lane < half, 0.0, scale).astype(BF16))

    @pl.when(i == 0)
    def _init():
        zeros = jnp.zeros((WINDOW, V7X_LANES), BF16)
        for kv in range(ATTN_KV_HEADS):
            for r0 in (0, T + WINDOW):
                k2[kv, r0:r0 + WINDOW, :] = zeros
                v2[kv, r0:r0 + WINDOW, 0:V7X_LANES] = zeros
                v2[kv, r0:r0 + WINDOW, V7X_LANES:2 * V7X_LANES] = jnp.ones((WINDOW, V7X_LANES), BF16)
        lane_r = lax.broadcasted_iota(jnp.int32, (rt, V7X_LANES), 1)

        def fill(t, _):
            src = pl.ds(pl.multiple_of(t * rt, rt), rt)
            dst = pl.ds(pl.multiple_of(t * rt, rt) + WINDOW, rt)
            kf = k_ref[src, :].astype(F32)
            vf = v_ref[src, :].astype(F32)
            for kv in range(ATTN_KV_HEADS):
                sel = (lane_r < half) if kv == 0 else (lane_r >= half)
                km = jnp.where(sel, kf, 0.0)
                vm = jnp.where(sel, vf, 0.0)
                k2[kv, dst, :] = (km + pltpu.roll(km, half, 1)).astype(BF16)
                v2[kv, dst, 0:V7X_LANES] = (vm + pltpu.roll(vm, half, 1)).astype(BF16)
                v2[kv, dst, V7X_LANES:2 * V7X_LANES] = jnp.ones((rt, V7X_LANES), BF16)
            return 0

        lax.fori_loop(0, T // rt, fill, 0)
        a = lax.broadcasted_iota(jnp.int32, (WINDOW, 3 * WINDOW), 0)
        j = lax.broadcasted_iota(jnp.int32, (WINDOW, 3 * WINDOW), 1)
        dist = jnp.abs(j - WINDOW - a)
        for h in range(ATTN_Q_HEADS):
            slope = 2.0 ** (-8.0 * (h + 1) / ATTN_Q_HEADS)
            bias[h] = jnp.where(dist <= WINDOW, -slope * dist.astype(F32), MASKED)

    jrow = lax.broadcasted_iota(jnp.int32, (1, 3 * WINDOW), 1)

    def window(qi):
        return pl.ds(pl.multiple_of((i * (tq // WINDOW) + qi) * WINDOW, WINDOW), 3 * WINDOW)

    def score(qi, h):
        c0 = (h // 2) * V7X_LANES
        qm = q_ref[qi * WINDOW:(qi + 1) * WINDOW, c0:c0 + V7X_LANES] * q_scale[h % 2]
        return _dot_nt(qm, k2[h // ATTN_GROUP, window(qi), :])

    def attend(qi, h, s, pos_bias):
        sink = sink_ref[h]
        s = s + bias[h] + pos_bias
        m = jnp.maximum(jnp.max(s, axis=-1, keepdims=True), sink)
        p = jnp.exp(s - m).astype(BF16)
        ov = _dot(p, v2[h // ATTN_GROUP, window(qi), :])
        return ov[:, :V7X_LANES] / (ov[:, V7X_LANES:] + jnp.exp(sink - m))

    nblk = tq // WINDOW
    scores = [score(0, h) for h in range(ATTN_Q_HEADS)]
    for qi in range(nblk):
        gi = i * nblk + qi
        first_valid = jnp.where(gi == 0, WINDOW, 0)
        end_valid = jnp.where(gi == nb - 1, 2 * WINDOW, 3 * WINDOW)
        outside = (jrow < first_valid) | (jrow >= end_valid)
        pos_bias = jnp.where(outside, MASKED, 0.0)
        tiles = []
        for h in range(ATTN_Q_HEADS):
            tiles.append(attend(qi, h, scores[h], pos_bias))
            if qi + 1 < nblk:
                scores[h] = score(qi + 1, h)
        for pair in range(ATTN_Q_HEADS // 2):
            c0 = pair * V7X_LANES
            o_ref[qi * WINDOW:(qi + 1) * WINDOW, c0:c0 + V7X_LANES] = \
                jnp.where(lane < half, tiles[2 * pair], tiles[2 * pair + 1]).astype(o_ref.dtype)


def _attention(y_at, sink, tq):
    B, T, _ = y_at.shape
    rt = min(256, T)
    return pl.pallas_call(
        functools.partial(_attn_body, T=T, tq=tq, rt=rt),
        grid=(B, T // tq),
        in_specs=[pl.BlockSpec(memory_space=pltpu.SMEM),
                  pl.BlockSpec((None, tq, ATTN_WIDTH), lambda b, i: (b, i, 0)),
                  pl.BlockSpec((None, T, KV_WIDTH), lambda b, i: (b, 0, ATTN_WIDTH // KV_WIDTH)),
                  pl.BlockSpec((None, T, KV_WIDTH), lambda b, i: (b, 0, ATTN_WIDTH // KV_WIDTH + 1))],
        out_specs=pl.BlockSpec((None, tq, ATTN_WIDTH), lambda b, i: (b, i, 0)),
        out_shape=jax.ShapeDtypeStruct((B, T, ATTN_WIDTH), BF16),
        scratch_shapes=[pltpu.VMEM((ATTN_KV_HEADS, T + 2 * WINDOW, V7X_LANES), BF16),
                        pltpu.VMEM((ATTN_KV_HEADS, T + 2 * WINDOW, 2 * V7X_LANES), BF16),
                        pltpu.VMEM((ATTN_Q_HEADS, WINDOW, 3 * WINDOW), F32)],
        compiler_params=_params(("arbitrary", "arbitrary")),
        name="window_attn",
    )(sink, y_at, y_at, y_at)


def _merge_body(x_ref, a_ref, b_ref, g_ref, pa_ref, pb_ref, wo_ref, gpost_ref, o_ref, *, d_model):
    parts = _row_parts(x_ref.shape[0])
    heads = range(a_ref.shape[0])
    ya = [_dot(jnp.concatenate([a_ref[h, r, :] for h in heads], axis=1), pa_ref[...]) for r in parts]
    yb = [_dot(b_ref[r, :], pb_ref[...]) for r in parts]
    mix = []
    for r, yap, ybp in zip(parts, ya, yb):
        g = g_ref[r, :].astype(F32)
        merged = jax.nn.sigmoid(g[:, :d_model]) * yap + jax.nn.sigmoid(g[:, d_model:]) * ybp
        mix.append(_dot(merged.astype(BF16), wo_ref[...]))
    for r, mp in zip(parts, mix):
        o_ref[r, :] = x_ref[r, :] + _rms(mp, gpost_ref[...])


def _merge(x, out_a, out_b, y_g, pa, pb, wo, gpost, tm):
    B, T, D = x.shape
    tok = lambda n: pl.BlockSpec((None, tm, n), lambda b, i: (b, i, 0))
    return pl.pallas_call(
        functools.partial(_merge_body, d_model=D),
        grid=(B, T // tm),
        in_specs=[tok(D),
                  pl.BlockSpec((None, out_a.shape[1], tm, out_a.shape[3]), lambda b, i: (b, 0, i, 0)),
                  tok(out_b.shape[-1]), tok(y_g.shape[-1]),
                  _resident(pa.shape), _resident(pb.shape), _resident(wo.shape), _resident((1, D))],
        out_specs=tok(D),
        out_shape=jax.ShapeDtypeStruct((B, T, D), F32),
        compiler_params=_params(("arbitrary", "arbitrary")),
        name="merge",
    )(x, out_a, out_b, y_g, pa, pb, wo, gpost)


def _split_w_in(w_in):
    D = w_in.shape[0]
    H = DN_HEADS
    o_small = 4 * DN_WIDTH
    o_attn = o_small + 4 * H
    o_gate = o_attn + ATTN_WIDTH + 2 * KV_WIDTH
    wdn = w_in[:, :o_small].astype(BF16)
    small = w_in[:, o_small:o_attn].reshape(D, 4, H)
    small = jnp.transpose(small, (0, 2, 1))
    wsr =jnp.pad(small, ((0, 0), (0, 0), (0, V7X_SUBLANES - 4))).reshape(D, H * V7X_SUBLANES).T.astype(BF16)
    wat = w_in[:, o_attn:o_gate].astype(BF16)
    wg = w_in[:, o_gate:].astype(BF16)
    return wdn, wsr, wat, wg


def _gate_consts(a_log, dt_bias):
    H = DN_HEADS
    both = jnp.stack([a_log, dt_bias], axis=0).astype(F32)
    per_head = jnp.transpose(both, (2, 1, 0))
    return jnp.zeros((H, V7X_SUBLANES, 2), F32).at[:, 2:4, :].set(per_head)


def _tile(n, pref):
    t = min(pref, n)
    assert n % t == 0
    return t


def _trunk(x, layers):
    T = x.shape[1]
    tm = _tile(T, DENSE_TILE_ROWS)
    tq = _tile(T, ATTN_TILE_ROWS)
    for p in layers:
        x = _ffn(x, p["f1_pre"], p["f1_post"], p["f1_w_in"], p["f1_w_out"], tm)
        y_dn, y_sr, y_at, y_g = _proj(x, p["m_pre"], p["wdn"], p["wsr"], p["wat"], p["wg"], tm)
        out_a = _deltanet(y_dn, y_sr, p["conv_w"], p["lr"], p["dn_norm_w"])
        out_b = _attention(y_at, p["sink"], tq)
        x = _merge(x, out_a, out_b, y_g, p["pa"], p["pb"], p["wo"], p["m_post"], tm)
        x = _ffn(x, p["f2_pre"], p["f2_post"], p["f2_w_in"], p["f2_w_out"], tm)
    return x


def kernel(x_prompt, x_sample, ffn1_norm_pre, ffn1_norm_post, ffn1_w_in, ffn1_w_out, mix_norm_pre, mix_norm_post, mix_w_in, dn_conv_w, dn_a_log, dn_dt_bias, dn_norm_w, attn_sink, w_branch_a, w_branch_b, mix_w_out, ffn2_norm_pre, ffn2_norm_post, ffn2_w_in, ffn2_w_out):
    depth = ffn1_w_in.shape[0]
    row = lambda v: v.reshape(1, -1).astype(F32)
    layers = []
    for l in range(depth):
        wdn, wsr, wat, wg = _split_w_in(mix_w_in[l])
        lr = _gate_consts(dn_a_log[l], dn_dt_bias[l])
        layers.append(dict(
            f1_pre=row(ffn1_norm_pre[l]), f1_post=row(ffn1_norm_post[l]),
            f1_w_in=ffn1_w_in[l].astype(BF16), f1_w_out=ffn1_w_out[l].astype(BF16),
            m_pre=row(mix_norm_pre[l]), m_post=row(mix_norm_post[l]),
            wdn=wdn, wsr=wsr, wat=wat, wg=wg,
            conv_w=dn_conv_w[l].astype(F32), lr=lr, dn_norm_w=row(dn_norm_w[l]),
            sink=attn_sink[l].astype(F32),
            pa=w_branch_a[l].astype(BF16), pb=w_branch_b[l].astype(BF16), wo=mix_w_out[l].astype(BF16),
            f2_pre=row(ffn2_norm_pre[l]), f2_post=row(ffn2_norm_post[l]),
            f2_w_in=ffn2_w_in[l].astype(BF16), f2_w_out=ffn2_w_out[l].astype(BF16),
        ))
    return (_trunk(x_prompt, layers), _trunk(x_sample, layers))
```

```python
import functools

import jax
import jax.numpy as jnp
from jax import lax
from jax.experimental import pallas as pl
from jax.experimental.pallas import tpu as pltpu

F32 = jnp.float32
BF16 = jnp.bfloat16
HIGHEST = lax.Precision.HIGHEST

NORM_EPS = 1e-6
DN_HEADS = 4
DN_HEAD_DIM = 128
DN_WIDTH = DN_HEADS * DN_HEAD_DIM
CONV_WIDTH = 5
ATTN_Q_HEADS = 8
ATTN_KV_HEADS = 2
ATTN_GROUP = ATTN_Q_HEADS // ATTN_KV_HEADS
ATTN_HEAD_DIM = 64
ATTN_WIDTH = ATTN_Q_HEADS * ATTN_HEAD_DIM
KV_WIDTH = ATTN_KV_HEADS * ATTN_HEAD_DIM
WINDOW = 128

V7X_LANES = 128
V7X_SUBLANES = 8
V7X_VMEM_LIMIT_BYTES = 56 * 1024 * 1024
DENSE_TILE_ROWS = 1024
ATTN_TILE_ROWS = 1024
PART_ROWS = 256

SUPER = 128
DN_GROUP = 8
assert SUPER == V7X_LANES and WINDOW == V7X_LANES and DN_HEAD_DIM == V7X_LANES
MASKED = -1e30

_EYE, _TRI0, _SREL0, _M8_0 = 0, 1, 3, 5
_MERGE_LEVELS = (3, 4, 5, 6)
_MERGE0 = 7
_NMASK = _MERGE0 + 2 * len(_MERGE_LEVELS)
_PREPARE_STAGES_BEFORE_SLOT_WRITE = 13


def _rms(x, g):
    return x * lax.rsqrt(jnp.mean(x * x, axis=-1, keepdims=True) + NORM_EPS) * g


def _dot(a, b):
    return jnp.dot(a, b, preferred_element_type=F32)


def _dot_nt(a, b):
    return lax.dot_general(a, b, (((1,), (1,)), ((), ())), preferred_element_type=F32)


def _dot_tn(a, b):
    return lax.dot_general(a, b, (((0,), (0,)), ((), ())), preferred_element_type=F32)


def _softplus(x):
    return jnp.maximum(x, 0.0) + jnp.log1p(jnp.exp(-jnp.abs(x)))


def _resident(shape):
    nd = len(shape)
    return pl.BlockSpec(shape, lambda *_: (0,) * nd, pipeline_mode=pl.Buffered(1))


def _params(sem):
    return pltpu.CompilerParams(dimension_semantics=sem, vmem_limit_bytes=V7X_VMEM_LIMIT_BYTES)


def _row_parts(tm):
    rows = PART_ROWS if tm % PART_ROWS == 0 else tm
    return [pl.ds(r0, rows) for r0 in range(0, tm, rows)]


def _ffn_body(x_ref, gpre_ref, gpost_ref, win_ref, wout_ref, o_ref, *, d_ff):
    parts = _row_parts(x_ref.shape[0])
    h = [_rms(x_ref[r, :], gpre_ref[...]).astype(BF16) for r in parts]
    gu = [_dot(hp, win_ref[...]) for hp in h]
    act = [(g[:, :d_ff] * jax.nn.sigmoid(g[:, :d_ff]) * g[:, d_ff:]).astype(BF16) for g in gu]
    y = [_dot(a, wout_ref[...]) for a in act]
    for r, yp in zip(parts, y):
        o_ref[r, :] = x_ref[r, :] + 0.5 * _rms(yp, gpost_ref[...])


def _ffn(x, gpre, gpost, w_in, w_out, tm):
    B, T, D = x.shape
    d_ff = w_out.shape[0]
    x2 = x.reshape(B * T, D)
    out = pl.pallas_call(
        functools.partial(_ffn_body, d_ff=d_ff),
        grid=(B * T // tm,),
        in_specs=[
            pl.BlockSpec((tm, D), lambda i: (i, 0)),
            _resident((1, D)),
            _resident((1, D)),
            _resident((D, 2 * d_ff)),
            _resident((d_ff, D)),
        ],
        out_specs=pl.BlockSpec((tm, D), lambda i: (i, 0)),
        out_shape=jax.ShapeDtypeStruct((B * T, D), F32),
        compiler_params=_params(("arbitrary",)),
        name="ffn",
    )(x2, gpre, gpost, w_in, w_out)
    return out.reshape(B, T, D)


def _proj_body(x_ref, g_ref, wdn_ref, wsr_ref, wat_ref, wg_ref, odn_ref, osr_ref, oat_ref, og_ref):
    for r in _row_parts(x_ref.shape[0]):
        u = _rms(x_ref[r, :], g_ref[...]).astype(BF16)
        odn_ref[r, :] = _dot(u, wdn_ref[...]).astype(odn_ref.dtype)
        osr_ref[:, r] = _dot_nt(wsr_ref[...], u)
        oat_ref[r, :] = _dot(u, wat_ref[...]).astype(oat_ref.dtype)
        og_ref[r, :] = _dot(u, wg_ref[...]).astype(og_ref.dtype)


def _proj(x, g, wdn, wsr, wat, wg, tm):
    B, T, D = x.shape
    tok = lambda n: pl.BlockSpec((None, tm, n), lambda b, i: (b, i, 0))
    nsr = wsr.shape[0]
    return pl.pallas_call(
        _proj_body,
        grid=(B, T // tm),
        in_specs=[tok(D), _resident((1, D)), _resident(wdn.shape),
                  _resident(wsr.shape), _resident(wat.shape), _resident(wg.shape)],
        out_specs=[tok(wdn.shape[1]),
                   pl.BlockSpec((None, nsr, tm), lambda b, i: (b, 0, i)),
                   tok(wat.shape[1]), tok(wg.shape[1])],
        out_shape=[
            jax.ShapeDtypeStruct((B, T, wdn.shape[1]), BF16),
            jax.ShapeDtypeStruct((B, nsr, T), F32),
            jax.ShapeDtypeStruct((B, T, wat.shape[1]), BF16),
            jax.ShapeDtypeStruct((B, T, wg.shape[1]), BF16),
        ],
        compiler_params=_params(("arbitrary", "arbitrary")),
        name="proj",
    )(x, g, wdn, wsr, wat, wg)


def _build_masks(msk):
    ii = lax.broadcasted_iota(jnp.int32, (SUPER, SUPER), 0)
    jj = lax.broadcasted_iota(jnp.int32, (SUPER, SUPER), 1)
    same = lambda s: jnp.right_shift(ii, s) == jnp.right_shift(jj, s)
    f = lambda c: jnp.where(c, 1.0, 0.0).astype(F32)
    msk[_EYE] = f(ii == jj)
    for d in range(2):
        after = (ii > jj) if d == 0 else (ii < jj)
        after_eq = (ii >= jj) if d == 0 else (ii <= jj)
        msk[_TRI0 + d] = f(after_eq)
        msk[_SREL0 + d] = f(after)
        msk[_M8_0 + d] = f(same(_MERGE_LEVELS[0]) & after)
        for n, lvl in enumerate(_MERGE_LEVELS):
            msk[_MERGE0 + 2 * n + d] = f(same(lvl + 1) & jnp.logical_not(same(lvl)) & after)


def _prepare_stages(chains, qn, kn, vn, gcol, grow, msk, bufs):
    mm = lambda a, b: _dot(a.astype(BF16), b.astype(BF16))
    eye = msk[_EYE]
    st = []
    for d, r0, slot in chains:
        rows = pl.ds(r0, SUPER)
        c = dict(d=d, slot=slot, rows=rows)
        c["kb"] = kn[rows, :].astype(BF16)
        c["vb"] = vn[rows, :].astype(BF16)
        grv = grow[:, rows]
        c["gc_r"] = grv[2 + d:3 + d, :]
        c["beta_r"] = grv[d:d + 1, :]
        c["qg"] = _dot_nt(jnp.concatenate([qn[rows, :].astype(BF16), c["kb"]], axis=0), c["kb"])
        st.append(c)
    yield
    for c in st:
        srel = msk[_SREL0 + c["d"]]
        dec = jnp.exp((gcol[2 + c["d"], c["rows"], :] - c["gc_r"]) * srel) * srel
        c["lm"] = c["qg"][SUPER:] * gcol[c["d"], c["rows"], :] * dec
        c["intra"] = (c["qg"][:SUPER] * (dec + eye)).astype(BF16)
        c["x"] = c["lm"] * msk[_M8_0 + c["d"]]
        c["x2"] = mm(c["x"], c["x"])
    yield
    for c in st:
        x34 = mm(c["x2"], jnp.concatenate([c["x"], c["x2"]], axis=1))
        c["x3"] = x34[:, :SUPER]
        c["x4"] = x34[:, SUPER:]
    yield
    for c in st:
        p1 = eye - c["x"] + c["x2"] - c["x3"]
        c["inv"] = p1 + mm(p1, c["x4"])
    yield
    for n, lvl in enumerate(_MERGE_LEVELS):
        blk = 1 << lvl
        later = lambda d, b: (b % 2 == 1) if d == 0 else (b % 2 == 0)
        pick = lambda a, d: jnp.concatenate(
            [a[b * blk:(b + 1) * blk] for b in range(SUPER // blk) if later(d, b)], axis=0)
        for c in st:
            c["t"] = mm(pick(c["inv"], c["d"]), c["lm"] * msk[_MERGE0 + 2 * n + c["d"]])
        yield
        for c in st:
            upd = pick(c["inv"], c["d"]) - mm(c["t"], c["inv"])
            pieces, taken = [], 0
            for b in range(SUPER // blk):
                if later(c["d"], b):
                    pieces.append(upd[taken:taken + blk])
                    taken += blk
                else:
                    pieces.append(c["inv"][b * blk:(b + 1) * blk])
            c["inv"] = jnp.concatenate(pieces, axis=0)
        yield
    for c in st:
        tb = c["inv"] * c["beta_r"]
        tbe = tb * jnp.exp(c["gc_r"])
        u = _dot(tb.astype(BF16), c["vb"])
        w = _dot(tbe.astype(BF16), c["kb"])
        c["wu"] = jnp.concatenate([w, u], axis=1).astype(BF16)
    yield
    for c in st:
        ap, nn, o0b, et = bufs[c["d"]]
        iw = _dot(c["intra"], c["wu"])
        gc_c = gcol[2 + c["d"], c["rows"], :]
        c["qp"] = (qn[c["rows"], :] * jnp.exp(gc_c) - iw[:, :DN_HEAD_DIM]).astype(BF16)
        o0b[c["slot"]] = iw[:, DN_HEAD_DIM:]
    yield
    for c in st:
        ap, nn, o0b, et = bufs[c["d"]]
        last = SUPER - 1 if c["d"] == 0 else 0
        gc_c = gcol[2 + c["d"], c["rows"], :]
        tot = gc_c[last:last + 1, :]
        kdec = (kn[c["rows"], :] * jnp.exp(tot - gc_c)).astype(BF16)
        mn = _dot_tn(kdec, c["wu"])
        ap[c["slot"]] = jnp.concatenate([mn[:, :DN_HEAD_DIM].astype(BF16), c["qp"]], axis=0)
        nn[c["slot"]] = mn[:, DN_HEAD_DIM:]
        et[c["slot"]] = jnp.broadcast_to(jnp.exp(tot), (V7X_SUBLANES, DN_HEAD_DIM))
    yield


def _scan_stages(rows_f, rows_b, bufs, s_f, s_b, oacc):
    state = [s_f[...], s_b[...]]
    for slot, r0s in enumerate(zip(rows_f, rows_b)):
        for d in range(2):
            ap, nn, o0b, et = bufs[d]
            s = state[d]
            r = _dot(ap[slot], s.astype(BF16))
            oacc[pl.ds(r0s[d], SUPER), :] += r[DN_HEAD_DIM:] + o0b[slot]
            s3 = s.reshape(DN_HEAD_DIM // V7X_SUBLANES, V7X_SUBLANES, DN_HEAD_DIM) * et[slot][None]
            state[d] = s3.reshape(DN_HEAD_DIM, DN_HEAD_DIM) - r[:DN_HEAD_DIM] + nn[slot]
        yield
    s_f[...] = state[0]
    s_b[...] = state[1]
    yield


def _interleave(*stage_generators):
    live = list(stage_generators)
    while live:
        for gen in list(live):
            try:
                next(gen)
            except StopIteration:
                live.remove(gen)


def _dn_body(q_ref, k_ref, v_ref, z_ref, gr_ref, cwq_ref, cwk_ref, cwv_ref, lr_ref, nw_ref, o_ref,
             pad, qn, kn, vn, gcol, grow, oacc, s_f, s_b, msk,
             ap_f, nn_f, o0_f, et_f, ap_b, nn_b, o0_b, et_b, *, T, rt, group):
    nt = T // rt
    ns = T // SUPER
    halo = V7X_SUBLANES
    nsub = V7X_SUBLANES
    _build_masks(msk)

    xr = gr_ref[...]
    lr = lr_ref[...]
    rowi = lax.broadcasted_iota(jnp.int32, xr.shape, 0)
    grow[...] = jnp.where(rowi < 2, jax.nn.sigmoid(xr),
                          -jnp.exp(lr[:, 0:1]) * _softplus(xr + lr[:, 1:2]))

    ngate = 4
    tri_bf = jnp.concatenate([msk[_TRI0 + 1], msk[_TRI0]], axis=1).astype(BF16)
    row_s = lax.broadcasted_iota(jnp.int32, (nsub, SUPER), 0)
    sel_r = lax.broadcasted_iota(jnp.int32, (4 * nsub, ngate * V7X_LANES), 0)
    sel_n = lax.broadcasted_iota(jnp.int32, (4 * nsub, ngate * V7X_LANES), 1)
    sel = jnp.where(((sel_r & (nsub - 1)) == jnp.right_shift(sel_n, 7)) & (sel_r < 3 * nsub), 1.0, 0.0).astype(BF16)

    def split3(x):
        hi = x.astype(BF16).astype(F32)
        r1 = x - hi
        mid = r1.astype(BF16).astype(F32)
        lo = r1 - mid
        return jnp.concatenate([hi, mid, lo, jnp.zeros_like(x)], axis=0).astype(BF16)

    cum_unroll = _largest_divisor(ns, 8)

    def cum_tile(i, _):
        tiles = []
        for uu in range(cum_unroll):
            rows = pl.ds(pl.multiple_of((i * cum_unroll + uu) * SUPER, SUPER), SUPER)
            grv = grow[:, rows]
            tiles.append((rows, grv, _dot(split3(grv), tri_bf)))
        cols = []
        for rows, grv, rs in tiles:
            rs = rs[:nsub] + rs[nsub:2 * nsub] + rs[2 * nsub:3 * nsub]
            gnew = jnp.where(row_s == 2, rs[:, :SUPER], jnp.where(row_s == 3, rs[:, SUPER:], grv))
            grow[:, rows] = gnew
            cols.append((rows, _dot_tn(split3(gnew), sel)))
        for rows, col in cols:
            for j in range(ngate):
                gcol[j, rows, :] = col[:, j * V7X_LANES:(j + 1) * V7X_LANES]
        return 0

    lax.fori_loop(0, ns // cum_unroll, cum_tile, 0)

    streams = ((q_ref, cwq_ref[...], qn, DN_HEAD_DIM ** -0.5), (k_ref, cwk_ref[...], kn, 1.0),
               (v_ref, cwv_ref[...], vn, None))
    for n in range(len(streams)):
        pad[n, 0:halo, :] = jnp.zeros((halo, DN_HEAD_DIM), F32)
        pad[n, T + halo:T + 2 * halo, :] = jnp.zeros((halo, DN_HEAD_DIM), F32)

    def fill(i, _):
        r = pl.multiple_of(i * rt, rt)
        for n, (src, _, _, _) in enumerate(streams):
            pad[n, pl.ds(r + halo, rt), :] = src[pl.ds(r, rt), :].astype(F32)
        oacc[pl.ds(r, rt), :] = jnp.zeros((rt, DN_HEAD_DIM), F32)
        return 0

    lax.fori_loop(0, nt, fill, 0, unroll=2)

    def conv_rows(r0):
        base = r0 + halo - CONV_WIDTH // 2
        for n, (_, cw, dst, scale) in enumerate(streams):
            acc = pad[n, pl.ds(base, rt), :] * cw[0:1, :]
            for j in range(1, CONV_WIDTH):
                acc = acc + pad[n, pl.ds(base + j, rt), :] * cw[j:j + 1, :]
            y = acc * jax.nn.sigmoid(acc)
            if scale is not None:
                y = y * (lax.rsqrt(jnp.sum(y * y, axis=-1, keepdims=True) + NORM_EPS) * scale)
            dst[pl.ds(r0, rt), :] = y

    nw = nw_ref[...]

    def out_rows(r0):
        r = pl.ds(r0, rt)
        z = z_ref[r, :].astype(F32)
        o_ref[r, :] = (_rms(oacc[r, :], nw) * (z * jax.nn.sigmoid(z))).astype(o_ref.dtype)

    def loop_rows(fn, lo, hi):
        def tile(i, _):
            fn(pl.multiple_of(i * rt, rt))
            return 0
        lax.fori_loop(lo // rt, hi // rt, tile, 0, unroll=2)

    def row_stages(fn, lo, hi):
        for r0 in range(lo, hi, rt):
            fn(r0)
            yield

    s_f[...] = jnp.zeros((DN_HEAD_DIM, DN_HEAD_DIM), F32)
    s_b[...] = jnp.zeros((DN_HEAD_DIM, DN_HEAD_DIM), F32)
    fwd_row = lambda g, uu: pl.multiple_of((g * group + uu) * SUPER, SUPER)
    bwd_row = lambda g, uu: pl.multiple_of((ns - 1 - (g * group + uu)) * SUPER, SUPER)

    bufs = ((ap_f, nn_f, o0_f, et_f), (ap_b, nn_b, o0_b, et_b))

    def prepare(g):
        chains = [(d, (fwd_row, bwd_row)[d](g, uu), uu) for uu in range(group) for d in range(2)]
        return _prepare_stages(chains, qn, kn, vn, gcol, grow, msk, bufs)

    def scan(g):
        return _scan_stages([fwd_row(g, uu) for uu in range(group)], [bwd_row(g, uu) for uu in range(group)],
                            bufs, s_f, s_b, oacc)

    n_groups = ns // group
    edge = group * SUPER
    assert edge % rt == 0
    loop_rows(conv_rows, 0, edge)
    if n_groups >= 2:
        loop_rows(conv_rows, T - edge, T)
    _interleave(prepare(0), row_stages(conv_rows, edge, T - edge))

    def group_step(g, _):
        _interleave(prepare(g + 1), scan(g))
        return 0

    lax.fori_loop(0, n_groups - 1, group_step, 0)
    _interleave(scan(n_groups - 1), row_stages(out_rows, edge, T - edge))
    loop_rows(out_rows, 0, edge)
    if n_groups >= 2:
        loop_rows(out_rows, T - edge, T)


def _largest_divisor(n, cap):
    return max(d for d in range(1, cap + 1) if n % d == 0)


def _deltanet(y_dn, y_sr, conv_w, lr, nw):
    B, T, _ = y_dn.shape
    H, hd = DN_HEADS, DN_HEAD_DIM
    rt = min(256, T)
    ns = T // SUPER
    group = _largest_divisor(ns, DN_GROUP)
    assert group + 1 <= _PREPARE_STAGES_BEFORE_SLOT_WRITE
    col = lambda off: pl.BlockSpec((None, T, hd), lambda b, h: (b, 0, off + h))
    cw = lambda off: pl.BlockSpec((CONV_WIDTH, hd), lambda b, h: (0, off + h))
    tok = pltpu.VMEM((T, hd), F32)
    slots = [pltpu.VMEM((group, hd + SUPER, hd), BF16), pltpu.VMEM((group, hd, hd), F32),
             pltpu.VMEM((group, SUPER, hd), F32), pltpu.VMEM((group, V7X_SUBLANES, hd), F32)]
    return pl.pallas_call(
        functools.partial(_dn_body, T=T, rt=rt, group=group),
        grid=(B, H),
        in_specs=[col(0), col(H), col(2 * H), col(3 * H),
                  pl.BlockSpec((None, V7X_SUBLANES, T), lambda b, h: (b, h, 0)),
                  cw(0), cw(H), cw(2 * H),
                  pl.BlockSpec((None, V7X_SUBLANES, 2), lambda b, h: (h, 0, 0)),
                  pl.BlockSpec((1, hd), lambda b, h: (0, 0))],
        out_specs=pl.BlockSpec((None, T, hd), lambda b, h: (b, 0, h)),
        out_shape=jax.ShapeDtypeStruct((B, T, DN_WIDTH), BF16),
        scratch_shapes=[pltpu.VMEM((3, T + 2 * V7X_SUBLANES, hd), F32), tok, tok, tok,
                        pltpu.VMEM((4, T, V7X_LANES), F32), pltpu.VMEM((V7X_SUBLANES, T), F32), tok,
                        pltpu.VMEM((hd, hd), F32), pltpu.VMEM((hd, hd), F32),
                        pltpu.VMEM((_NMASK, SUPER, SUPER), F32)] + slots + slots,
        compiler_params=_params(("arbitrary", "arbitrary")),
        name="deltanet",
    )(y_dn, y_dn, y_dn, y_dn, y_sr, conv_w, conv_w, conv_w, lr, nw)


def _attn_body(sink_ref, q_ref, k_ref, v_ref, o_ref, k2, v2, bias, *, T, tq, rt):
    i = pl.program_id(1)
    nb = T // WINDOW
    half = ATTN_HEAD_DIM
    lane = lax.broadcasted_iota(jnp.int32, (WINDOW, V7X_LANES), 1)
    scale = ATTN_HEAD_DIM ** -0.5
    assert scale == 2.0 ** -3
    q_scale = (jnp.where(lane < half, scale, 0.0).astype(BF16), jnp.where(lane < half, 0.0, scale).astype(BF16))

    @pl.when(i == 0)
    def _init():
        zeros = jnp.zeros((WINDOW, V7X_LANES), BF16)
        for kv in range(ATTN_KV_HEADS):
            for r0 in (0, T + WINDOW):
                k2[kv, r0:r0 + WINDOW, :] = zeros
                v2[kv, r0:r0 + WINDOW, 0:V7X_LANES] = zeros
                v2[kv, r0:r0 + WINDOW, V7X_LANES:2 * V7X_LANES] = jnp.ones((WINDOW, V7X_LANES), BF16)
        lane_r = lax.broadcasted_iota(jnp.int32, (rt, V7X_LANES), 1)

        def fill(t, _):
            src = pl.ds(pl.multiple_of(t * rt, rt), rt)
            dst = pl.ds(pl.multiple_of(t * rt, rt) + WINDOW, rt)
            kf = k_ref[src, :].astype(F32)
            vf = v_ref[src, :].astype(F32)
            for kv in range(ATTN_KV_HEADS):
                sel = (lane_r < half) if kv == 0 else (lane_r >= half)
                km = jnp.where(sel, kf, 0.0)
                vm = jnp.where(sel, vf, 0.0)
                k2[kv, dst, :] = (km + pltpu.roll(km, half, 1)).astype(BF16)
                v2[kv, dst, 0:V7X_LANES] = (vm + pltpu.roll(vm, half, 1)).astype(BF16)
                v2[kv, dst, V7X_LANES:2 * V7X_LANES] = jnp.ones((rt, V7X_LANES), BF16)
            return 0

        lax.fori_loop(0, T // rt, fill, 0)
        a = lax.broadcasted_iota(jnp.int32, (WINDOW, 3 * WINDOW), 0)
        j = lax.broadcasted_iota(jnp.int32, (WINDOW, 3 * WINDOW), 1)
        dist = jnp.abs(j - WINDOW - a)
        for h in range(ATTN_Q_HEADS):
            slope = 2.0 ** (-8.0 * (h + 1) / ATTN_Q_HEADS)
            bias[h] = jnp.where(dist <= WINDOW, -slope * dist.astype(F32), MASKED)

    jrow = lax.broadcasted_iota(jnp.int32, (1, 3 * WINDOW), 1)

    def window(qi):
        return pl.ds(pl.multiple_of((i * (tq // WINDOW) + qi) * WINDOW, WINDOW), 3 * WINDOW)

    def score(qi, h):
        c0 = (h // 2) * V7X_LANES
        qm = q_ref[qi * WINDOW:(qi + 1) * WINDOW, c0:c0 + V7X_LANES] * q_scale[h % 2]
        return _dot_nt(qm, k2[h // ATTN_GROUP, window(qi), :])

    def attend(qi, h, s, pos_bias):
        sink = sink_ref[h]
        s = s + bias[h] + pos_bias
        m = jnp.maximum(jnp.max(s, axis=-1, keepdims=True), sink)
        p = jnp.exp(s - m).astype(BF16)
        ov = _dot(p, v2[h // ATTN_GROUP, window(qi), :])
        return ov[:, :V7X_LANES] / (ov[:, V7X_LANES:] + jnp.exp(sink - m))

    nblk = tq // WINDOW
    scores = [score(0, h) for h in range(ATTN_Q_HEADS)]
    for qi in range(nblk):
        gi = i * nblk + qi
        first_valid = jnp.where(gi == 0, WINDOW, 0)
        end_valid = jnp.where(gi == nb - 1, 2 * WINDOW, 3 * WINDOW)
        outside = (jrow < first_valid) | (jrow >= end_valid)
        pos_bias = jnp.where(outside, MASKED, 0.0)
        tiles = []
        for h in range(ATTN_Q_HEADS):
            tiles.append(attend(qi, h, scores[h], pos_bias))
            if qi + 1 < nblk:
                scores[h] = score(qi + 1, h)
        for pair in range(ATTN_Q_HEADS // 2):
            c0 = pair * V7X_LANES
            o_ref[qi * WINDOW:(qi + 1) * WINDOW, c0:c0 + V7X_LANES] = \
                jnp.where(lane < half, tiles[2 * pair], tiles[2 * pair + 1]).astype(o_ref.dtype)


def _attention(y_at, sink, tq):
    B, T, _ = y_at.shape
    rt = min(256, T)
    return pl.pallas_call(
        functools.partial(_attn_body, T=T, tq=tq, rt=rt),
        grid=(B, T // tq),
        in_specs=[pl.BlockSpec(memory_space=pltpu.SMEM),
                  pl.BlockSpec((None, tq, ATTN_WIDTH), lambda b, i: (b, i, 0)),
                  pl.BlockSpec((None, T, KV_WIDTH), lambda b, i: (b, 0, ATTN_WIDTH // KV_WIDTH)),
                  pl.BlockSpec((None, T, KV_WIDTH), lambda b, i: (b, 0, ATTN_WIDTH // KV_WIDTH + 1))],
        out_specs=pl.BlockSpec((None, tq, ATTN_WIDTH), lambda b, i: (b, i, 0)),
        out_shape=jax.ShapeDtypeStruct((B, T, ATTN_WIDTH), BF16),
        scratch_shapes=[pltpu.VMEM((ATTN_KV_HEADS, T + 2 * WINDOW, V7X_LANES), BF16),
                        pltpu.VMEM((ATTN_KV_HEADS, T + 2 * WINDOW, 2 * V7X_LANES), BF16),
                        pltpu.VMEM((ATTN_Q_HEADS, WINDOW, 3 * WINDOW), F32)],
        compiler_params=_params(("arbitrary", "arbitrary")),
        name="window_attn",
    )(sink, y_at, y_at, y_at)


def _merge_body(x_ref, a_ref, b_ref, g_ref, pa_ref, pb_ref, wo_ref, gpost_ref, o_ref, *, d_model):
    parts = _row_parts(x_ref.shape[0])
    ya = [_dot(a_ref[r, :], pa_ref[...]) for r in parts]
    yb = [_dot(b_ref[r, :], pb_ref[...]) for r in parts]
    mix = []
    for r, yap, ybp in zip(parts, ya, yb):
        g = g_ref[r, :].astype(F32)
        merged = jax.nn.sigmoid(g[:, :d_model]) * yap + jax.nn.sigmoid(g[:, d_model:]) * ybp
        mix.append(_dot(merged.astype(BF16), wo_ref[...]))
    for r, mp in zip(parts, mix):
        o_ref[r, :] = x_ref[r, :] + _rms(mp, gpost_ref[...])


def _merge(x, out_a, out_b, y_g, pa, pb, wo, gpost, tm):
    B, T, D = x.shape
    tok = lambda n: pl.BlockSpec((None, tm, n), lambda b, i: (b, i, 0))
    return pl.pallas_call(
        functools.partial(_merge_body, d_model=D),
        grid=(B, T // tm),
        in_specs=[tok(D), tok(out_a.shape[-1]), tok(out_b.shape[-1]), tok(y_g.shape[-1]),
                  _resident(pa.shape), _resident(pb.shape), _resident(wo.shape), _resident((1, D))],
        out_specs=tok(D),
        out_shape=jax.ShapeDtypeStruct((B, T, D), F32),
        compiler_params=_params(("arbitrary", "arbitrary")),
        name="merge",
    )(x, out_a, out_b, y_g, pa, pb, wo, gpost)


def _split_w_in(w_in):
    D = w_in.shape[0]
    H = DN_HEADS
    o_small = 4 * DN_WIDTH
    o_attn = o_small + 4 * H
    o_gate = o_attn + ATTN_WIDTH + 2 * KV_WIDTH
    wdn = w_in[:, :o_small].astype(BF16)
    small = w_in[:, o_small:o_attn].reshape(D, 4, H)
    small = jnp.transpose(small, (0, 2, 1))
    wsr =jnp.pad(small, ((0, 0), (0, 0), (0, V7X_SUBLANES - 4))).reshape(D, H * V7X_SUBLANES).T.astype(BF16)
    wat = w_in[:, o_attn:o_gate].astype(BF16)
    wg = w_in[:, o_gate:].astype(BF16)
    return wdn, wsr, wat, wg


def _gate_consts(a_log, dt_bias):
    H = DN_HEADS
    both = jnp.stack([a_log, dt_bias], axis=0).astype(F32)
    per_head = jnp.transpose(both, (2, 1, 0))
    return jnp.zeros((H, V7X_SUBLANES, 2), F32).at[:, 2:4, :].set(per_head)


def _tile(n, pref):
    t = min(pref, n)
    assert n % t == 0
    return t


def _trunk(x, layers):
    T = x.shape[1]
    tm = _tile(T, DENSE_TILE_ROWS)
    tq = _tile(T, ATTN_TILE_ROWS)
    for p in layers:
        x = _ffn(x, p["f1_pre"], p["f1_post"], p["f1_w_in"], p["f1_w_out"], tm)
        y_dn, y_sr, y_at, y_g = _proj(x, p["m_pre"], p["wdn"], p["wsr"], p["wat"], p["wg"], tm)
        out_a = _deltanet(y_dn, y_sr, p["conv_w"], p["lr"], p["dn_norm_w"])
        out_b = _attention(y_at, p["sink"], tq)
        x = _merge(x, out_a, out_b, y_g, p["pa"], p["pb"], p["wo"], p["m_post"], tm)
        x = _ffn(x, p["f2_pre"], p["f2_post"], p["f2_w_in"], p["f2_w_out"], tm)
    return x


def kernel(x_prompt, x_sample, ffn1_norm_pre, ffn1_norm_post, ffn1_w_in, ffn1_w_out, mix_norm_pre, mix_norm_post, mix_w_in, dn_conv_w, dn_a_log, dn_dt_bias, dn_norm_w, attn_sink, w_branch_a, w_branch_b, mix_w_out, ffn2_norm_pre, ffn2_norm_post, ffn2_w_in, ffn2_w_out):
    depth = ffn1_w_in.shape[0]
    row = lambda v: v.reshape(1, -1).astype(F32)
    layers = []
    for l in range(depth):
        wdn, wsr, wat, wg = _split_w_in(mix_w_in[l])
        lr = _gate_consts(dn_a_log[l], dn_dt_bias[l])
        layers.append(dict(
            f1_pre=row(ffn1_norm_pre[l]), f1_post=row(ffn1_norm_post[l]),
            f1_w_in=ffn1_w_in[l].astype(BF16), f1_w_out=ffn1_w_out[l].astype(BF16),
            m_pre=row(mix_norm_pre[l]), m_post=row(mix_norm_post[l]),
            wdn=wdn, wsr=wsr, wat=wat, wg=wg,
            conv_w=dn_conv_w[l].astype(F32), lr=lr, dn_norm_w=row(dn_norm_w[l]),
            sink=attn_sink[l].astype(F32),
            pa=w_branch_a[l].astype(BF16), pb=w_branch_b[l].astype(BF16), wo=mix_w_out[l].astype(BF16),
            f2_pre=row(ffn2_norm_pre[l]), f2_post=row(ffn2_norm_post[l]),
            f2_w_in=ffn2_w_in[l].astype(BF16), f2_w_out=ffn2_w_out[l].astype(BF16),
        ))
    return (_trunk(x_prompt, layers), _trunk(x_sample, layers))
```

```python
import functools

import jax
import jax.numpy as jnp
from jax import lax
from jax.experimental import pallas as pl
from jax.experimental.pallas import tpu as pltpu

F32 = jnp.float32
BF16 = jnp.bfloat16
HIGHEST = lax.Precision.HIGHEST

NORM_EPS = 1e-6
DN_HEADS = 4
DN_HEAD_DIM = 128
DN_WIDTH = DN_HEADS * DN_HEAD_DIM
CONV_WIDTH = 5
ATTN_Q_HEADS = 8
ATTN_KV_HEADS = 2
ATTN_GROUP = ATTN_Q_HEADS // ATTN_KV_HEADS
ATTN_HEAD_DIM = 64
ATTN_WIDTH = ATTN_Q_HEADS * ATTN_HEAD_DIM
KV_WIDTH = ATTN_KV_HEADS * ATTN_HEAD_DIM
WINDOW = 128

V7X_LANES = 128
V7X_SUBLANES = 8
V7X_VMEM_LIMIT_BYTES = 56 * 1024 * 1024
DENSE_TILE_ROWS = 1024
ATTN_TILE_ROWS = 1024
PART_ROWS = 256

SUPER = 128
DN_GROUP = 8
assert SUPER == V7X_LANES and WINDOW == V7X_LANES and DN_HEAD_DIM == V7X_LANES
MASKED = -1e30

_EYE, _TRI0, _SREL0, _M8_0 = 0, 1, 3, 5
_MERGE_LEVELS = (3, 4, 5, 6)
_MERGE0 = 7
_NMASK = _MERGE0 + 2 * len(_MERGE_LEVELS)
_PREPARE_STAGES_BEFORE_SLOT_WRITE = 13


def _rms(x, g):
    return x * lax.rsqrt(jnp.mean(x * x, axis=-1, keepdims=True) + NORM_EPS) * g


def _dot(a, b):
    return jnp.dot(a, b, preferred_element_type=F32)


def _dot_nt(a, b):
    return lax.dot_general(a, b, (((1,), (1,)), ((), ())), preferred_element_type=F32)


def _dot_tn(a, b):
    return lax.dot_general(a, b, (((0,), (0,)), ((), ())), preferred_element_type=F32)


def _softplus(x):
    return jnp.maximum(x, 0.0) + jnp.log1p(jnp.exp(-jnp.abs(x)))


def _resident(shape):
    nd = len(shape)
    return pl.BlockSpec(shape, lambda *_: (0,) * nd, pipeline_mode=pl.Buffered(1))


def _params(sem):
    return pltpu.CompilerParams(dimension_semantics=sem, vmem_limit_bytes=V7X_VMEM_LIMIT_BYTES)


def _row_parts(tm):
    rows = PART_ROWS if tm % PART_ROWS == 0 else tm
    return [pl.ds(r0, rows) for r0 in range(0, tm, rows)]


def _ffn_body(x_ref, gpre_ref, gpost_ref, win_ref, wout_ref, o_ref, *, d_ff):
    parts = _row_parts(x_ref.shape[0])
    h = [_rms(x_ref[r, :], gpre_ref[...]).astype(BF16) for r in parts]
    gu = [_dot(hp, win_ref[...]) for hp in h]
    act = [(g[:, :d_ff] * jax.nn.sigmoid(g[:, :d_ff]) * g[:, d_ff:]).astype(BF16) for g in gu]
    y = [_dot(a, wout_ref[...]) for a in act]
    for r, yp in zip(parts, y):
        o_ref[r, :] = x_ref[r, :] + 0.5 * _rms(yp, gpost_ref[...])


def _ffn(x, gpre, gpost, w_in, w_out, tm):
    B, T, D = x.shape
    d_ff = w_out.shape[0]
    x2 = x.reshape(B * T, D)
    out = pl.pallas_call(
        functools.partial(_ffn_body, d_ff=d_ff),
        grid=(B * T // tm,),
        in_specs=[
            pl.BlockSpec((tm, D), lambda i: (i, 0)),
            _resident((1, D)),
            _resident((1, D)),
            _resident((D, 2 * d_ff)),
            _resident((d_ff, D)),
        ],
        out_specs=pl.BlockSpec((tm, D), lambda i: (i, 0)),
        out_shape=jax.ShapeDtypeStruct((B * T, D), F32),
        compiler_params=_params(("arbitrary",)),
        name="ffn",
    )(x2, gpre, gpost, w_in, w_out)
    return out.reshape(B, T, D)


def _proj_body(x_ref, g_ref, wdn_ref, wsr_ref, wat_ref, wg_ref, odn_ref, osr_ref, oat_ref, og_ref):
    for r in _row_parts(x_ref.shape[0]):
        u = _rms(x_ref[r, :], g_ref[...]).astype(BF16)
        odn_ref[r, :] = _dot(u, wdn_ref[...]).astype(odn_ref.dtype)
        osr_ref[:, r] = _dot_nt(wsr_ref[...], u)
        oat_ref[r, :] = _dot(u, wat_ref[...]).astype(oat_ref.dtype)
        og_ref[r, :] = _dot(u, wg_ref[...]).astype(og_ref.dtype)


def _proj(x, g, wdn, wsr, wat, wg, tm):
    B, T, D = x.shape
    tok = lambda n: pl.BlockSpec((None, tm, n), lambda b, i: (b, i, 0))
    nsr = wsr.shape[0]
    return pl.pallas_call(
        _proj_body,
        grid=(B, T // tm),
        in_specs=[tok(D), _resident((1, D)), _resident(wdn.shape),
                  _resident(wsr.shape), _resident(wat.shape), _resident(wg.shape)],
        out_specs=[tok(wdn.shape[1]),
                   pl.BlockSpec((None, nsr, tm), lambda b, i: (b, 0, i)),
                   tok(wat.shape[1]), tok(wg.shape[1])],
        out_shape=[
            jax.ShapeDtypeStruct((B, T, wdn.shape[1]), BF16),
            jax.ShapeDtypeStruct((B, nsr, T), F32),
            jax.ShapeDtypeStruct((B, T, wat.shape[1]), BF16),
            jax.ShapeDtypeStruct((B, T, wg.shape[1]), BF16),
        ],
        compiler_params=_params(("arbitrary", "arbitrary")),
        name="proj",
    )(x, g, wdn, wsr, wat, wg)


def _build_masks(msk):
    ii = lax.broadcasted_iota(jnp.int32, (SUPER, SUPER), 0)
    jj = lax.broadcasted_iota(jnp.int32, (SUPER, SUPER), 1)
    same = lambda s: jnp.right_shift(ii, s) == jnp.right_shift(jj, s)
    f = lambda c: jnp.where(c, 1.0, 0.0).astype(F32)
    msk[_EYE] = f(ii == jj)
    for d in range(2):
        after = (ii > jj) if d == 0 else (ii < jj)
        after_eq = (ii >= jj) if d == 0 else (ii <= jj)
        msk[_TRI0 + d] = f(after_eq)
        msk[_SREL0 + d] = f(after)
        msk[_M8_0 + d] = f(same(_MERGE_LEVELS[0]) & after)
        for n, lvl in enumerate(_MERGE_LEVELS):
            msk[_MERGE0 + 2 * n + d] = f(same(lvl + 1) & jnp.logical_not(same(lvl)) & after)


def _prepare_stages(chains, qn, kn, vn, gcol, grow, msk, bufs):
    mm = lambda a, b: _dot(a.astype(BF16), b.astype(BF16))
    eye = msk[_EYE]
    st = []
    for d, r0, slot in chains:
        rows = pl.ds(r0, SUPER)
        c = dict(d=d, slot=slot, rows=rows)
        c["kb"] = kn[rows, :].astype(BF16)
        c["vb"] = vn[rows, :].astype(BF16)
        grv = grow[:, rows]
        c["gc_r"] = grv[2 + d:3 + d, :]
        c["beta_r"] = grv[d:d + 1, :]
        c["qg"] = _dot_nt(jnp.concatenate([qn[rows, :].astype(BF16), c["kb"]], axis=0), c["kb"])
        st.append(c)
    yield
    for c in st:
        srel = msk[_SREL0 + c["d"]]
        dec = jnp.exp((gcol[2 + c["d"], c["rows"], :] - c["gc_r"]) * srel) * srel
        c["lm"] = c["qg"][SUPER:] * gcol[c["d"], c["rows"], :] * dec
        c["intra"] = (c["qg"][:SUPER] * (dec + eye)).astype(BF16)
        c["x"] = c["lm"] * msk[_M8_0 + c["d"]]
        c["x2"] = mm(c["x"], c["x"])
    yield
    for c in st:
        x34 = mm(c["x2"], jnp.concatenate([c["x"], c["x2"]], axis=1))
        c["x3"] = x34[:, :SUPER]
        c["x4"] = x34[:, SUPER:]
    yield
    for c in st:
        p1 = eye - c["x"] + c["x2"] - c["x3"]
        c["inv"] = p1 + mm(p1, c["x4"])
    yield
    for n, lvl in enumerate(_MERGE_LEVELS):
        blk = 1 << lvl
        later = lambda d, b: (b % 2 == 1) if d == 0 else (b % 2 == 0)
        pick = lambda a, d: jnp.concatenate(
            [a[b * blk:(b + 1) * blk] for b in range(SUPER // blk) if later(d, b)], axis=0)
        for c in st:
            c["t"] = mm(pick(c["inv"], c["d"]), c["lm"] * msk[_MERGE0 + 2 * n + c["d"]])
        yield
        for c in st:
            upd = pick(c["inv"], c["d"]) - mm(c["t"], c["inv"])
            pieces, taken = [], 0
            for b in range(SUPER // blk):
                if later(c["d"], b):
                    pieces.append(upd[taken:taken + blk])
                    taken += blk
                else:
                    pieces.append(c["inv"][b * blk:(b + 1) * blk])
            c["inv"] = jnp.concatenate(pieces, axis=0)
        yield
    for c in st:
        tb = c["inv"] * c["beta_r"]
        tbe = tb * jnp.exp(c["gc_r"])
        u = _dot(tb.astype(BF16), c["vb"])
        w = _dot(tbe.astype(BF16), c["kb"])
        c["wu"] = jnp.concatenate([w, u], axis=1).astype(BF16)
    yield
    for c in st:
        ap, nn, o0b, et = bufs[c["d"]]
        iw = _dot(c["intra"], c["wu"])
        gc_c = gcol[2 + c["d"], c["rows"], :]
        c["qp"] = (qn[c["rows"], :] * jnp.exp(gc_c) - iw[:, :DN_HEAD_DIM]).astype(BF16)
        o0b[c["slot"]] = iw[:, DN_HEAD_DIM:]
    yield
    for c in st:
        ap, nn, o0b, et = bufs[c["d"]]
        last = SUPER - 1 if c["d"] == 0 else 0
        gc_c = gcol[2 + c["d"], c["rows"], :]
        tot = gc_c[last:last + 1, :]
        kdec = (kn[c["rows"], :] * jnp.exp(tot - gc_c)).astype(BF16)
        mn = _dot_tn(kdec, c["wu"])
        ap[c["slot"]] = jnp.concatenate([mn[:, :DN_HEAD_DIM].astype(BF16), c["qp"]], axis=0)
        nn[c["slot"]] = mn[:, DN_HEAD_DIM:]
        et[c["slot"]] = jnp.broadcast_to(jnp.exp(tot), (V7X_SUBLANES, DN_HEAD_DIM))
    yield


def _scan_stages(rows_f, rows_b, bufs, s_f, s_b, oacc):
    state = [s_f[...], s_b[...]]
    for slot, r0s in enumerate(zip(rows_f, rows_b)):
        for d in range(2):
            ap, nn, o0b, et = bufs[d]
            s = state[d]
            r = _dot(ap[slot], s.astype(BF16))
            oacc[pl.ds(r0s[d], SUPER), :] += r[DN_HEAD_DIM:] + o0b[slot]
            s3 = s.reshape(DN_HEAD_DIM // V7X_SUBLANES, V7X_SUBLANES, DN_HEAD_DIM) * et[slot][None]
            state[d] = s3.reshape(DN_HEAD_DIM, DN_HEAD_DIM) - r[:DN_HEAD_DIM] + nn[slot]
        yield
    s_f[...] = state[0]
    s_b[...] = state[1]
    yield


def _interleave(*stage_generators):
    live = list(stage_generators)
    while live:
        for gen in list(live):
            try:
                next(gen)
            except StopIteration:
                live.remove(gen)


def _dn_body(q_ref, k_ref, v_ref, z_ref, gr_ref, cwq_ref, cwk_ref, cwv_ref, lr_ref, nw_ref, o_ref,
             pad, qn, kn, vn, gcol, grow, oacc, s_f, s_b, msk,
             ap_f, nn_f, o0_f, et_f, ap_b, nn_b, o0_b, et_b, *, T, rt, group):
    nt = T // rt
    ns = T // SUPER
    halo = V7X_SUBLANES
    nsub = V7X_SUBLANES
    _build_masks(msk)

    xr = gr_ref[...]
    lr = lr_ref[...]
    rowi = lax.broadcasted_iota(jnp.int32, xr.shape, 0)
    grow[...] = jnp.where(rowi < 2, jax.nn.sigmoid(xr),
                          -jnp.exp(lr[:, 0:1]) * _softplus(xr + lr[:, 1:2]))

    ngate = 4
    tri_bf = jnp.concatenate([msk[_TRI0 + 1], msk[_TRI0]], axis=1).astype(BF16)
    row_s = lax.broadcasted_iota(jnp.int32, (nsub, SUPER), 0)
    sel_r = lax.broadcasted_iota(jnp.int32, (4 * nsub, ngate * V7X_LANES), 0)
    sel_n = lax.broadcasted_iota(jnp.int32, (4 * nsub, ngate * V7X_LANES), 1)
    sel = jnp.where(((sel_r & (nsub - 1)) == jnp.right_shift(sel_n, 7)) & (sel_r < 3 * nsub), 1.0, 0.0).astype(BF16)

    def split3(x):
        hi = x.astype(BF16).astype(F32)
        r1 = x - hi
        mid = r1.astype(BF16).astype(F32)
        lo = r1 - mid
        return jnp.concatenate([hi, mid, lo, jnp.zeros_like(x)], axis=0).astype(BF16)

    cum_unroll = _largest_divisor(ns, 16)

    def cum_tile(i, _):
        tiles = []
        for uu in range(cum_unroll):
            rows = pl.ds(pl.multiple_of((i * cum_unroll + uu) * SUPER, SUPER), SUPER)
            grv = grow[:, rows]
            tiles.append((rows, grv, _dot(split3(grv), tri_bf)))
        cols = []
        for rows, grv, rs in tiles:
            rs = rs[:nsub] + rs[nsub:2 * nsub] + rs[2 * nsub:3 * nsub]
            gnew = jnp.where(row_s == 2, rs[:, :SUPER], jnp.where(row_s == 3, rs[:, SUPER:], grv))
            grow[:, rows] = gnew
            cols.append((rows, _dot_tn(split3(gnew), sel)))
        for rows, col in cols:
            for j in range(ngate):
                gcol[j, rows, :] = col[:, j * V7X_LANES:(j + 1) * V7X_LANES]
        return 0

    lax.fori_loop(0, ns // cum_unroll, cum_tile, 0)

    streams = ((q_ref, cwq_ref[...], qn, DN_HEAD_DIM ** -0.5), (k_ref, cwk_ref[...], kn, 1.0),
               (v_ref, cwv_ref[...], vn, None))
    for n in range(len(streams)):
        pad[n, 0:halo, :] = jnp.zeros((halo, DN_HEAD_DIM), F32)
        pad[n, T + halo:T + 2 * halo, :] = jnp.zeros((halo, DN_HEAD_DIM), F32)

    def fill(i, _):
        r = pl.multiple_of(i * rt, rt)
        for n, (src, _, _, _) in enumerate(streams):
            pad[n, pl.ds(r + halo, rt), :] = src[pl.ds(r, rt), :].astype(F32)
        oacc[pl.ds(r, rt), :] = jnp.zeros((rt, DN_HEAD_DIM), F32)
        return 0

    lax.fori_loop(0, nt, fill, 0, unroll=2)

    def conv_rows(r0):
        base = r0 + halo - CONV_WIDTH // 2
        for n, (_, cw, dst, scale) in enumerate(streams):
            acc = pad[n, pl.ds(base, rt), :] * cw[0:1, :]
            for j in range(1, CONV_WIDTH):
                acc = acc + pad[n, pl.ds(base + j, rt), :] * cw[j:j + 1, :]
            y = acc * jax.nn.sigmoid(acc)
            if scale is not None:
                y = y * (lax.rsqrt(jnp.sum(y * y, axis=-1, keepdims=True) + NORM_EPS) * scale)
            dst[pl.ds(r0, rt), :] = y

    nw = nw_ref[...]

    def out_rows(r0):
        r = pl.ds(r0, rt)
        z = z_ref[r, :].astype(F32)
        o_ref[r, :] = (_rms(oacc[r, :], nw) * (z * jax.nn.sigmoid(z))).astype(o_ref.dtype)

    def loop_rows(fn, lo, hi):
        def tile(i, _):
            fn(pl.multiple_of(i * rt, rt))
            return 0
        lax.fori_loop(lo // rt, hi // rt, tile, 0, unroll=2)

    def row_stages(fn, lo, hi):
        for r0 in range(lo, hi, rt):
            fn(r0)
            yield

    s_f[...] = jnp.zeros((DN_HEAD_DIM, DN_HEAD_DIM), F32)
    s_b[...] = jnp.zeros((DN_HEAD_DIM, DN_HEAD_DIM), F32)
    fwd_row = lambda g, uu: pl.multiple_of((g * group + uu) * SUPER, SUPER)
    bwd_row = lambda g, uu: pl.multiple_of((ns - 1 - (g * group + uu)) * SUPER, SUPER)

    bufs = ((ap_f, nn_f, o0_f, et_f), (ap_b, nn_b, o0_b, et_b))

    def prepare(g):
        chains = [(d, (fwd_row, bwd_row)[d](g, uu), uu) for uu in range(group) for d in range(2)]
        return _prepare_stages(chains, qn, kn, vn, gcol, grow, msk, bufs)

    def scan(g):
        return _scan_stages([fwd_row(g, uu) for uu in range(group)], [bwd_row(g, uu) for uu in range(group)],
                            bufs, s_f, s_b, oacc)

    n_groups = ns // group
    edge = group * SUPER
    assert edge % rt == 0
    loop_rows(conv_rows, 0, edge)
    if n_groups >= 2:
        loop_rows(conv_rows, T - edge, T)
    _interleave(prepare(0), row_stages(conv_rows, edge, T - edge))

    def group_step(g, _):
        _interleave(prepare(g + 1), scan(g))
        return 0

    lax.fori_loop(0, n_groups - 1, group_step, 0)
    _interleave(scan(n_groups - 1), row_stages(out_rows, edge, T - edge))
    loop_rows(out_rows, 0, edge)
    if n_groups >= 2:
        loop_rows(out_rows, T - edge, T)


def _largest_divisor(n, cap):
    return max(d for d in range(1, cap + 1) if n % d == 0)


def _deltanet(y_dn, y_sr, conv_w, lr, nw):
    B, T, _ = y_dn.shape
    H, hd = DN_HEADS, DN_HEAD_DIM
    rt = min(256, T)
    ns = T // SUPER
    group = _largest_divisor(ns, DN_GROUP)
    assert group + 1 <= _PREPARE_STAGES_BEFORE_SLOT_WRITE
    col = lambda off: pl.BlockSpec((None, T, hd), lambda b, h: (b, 0, off + h))
    cw = lambda off: pl.BlockSpec((CONV_WIDTH, hd), lambda b, h: (0, off + h))
    tok = pltpu.VMEM((T, hd), F32)
    slots = [pltpu.VMEM((group, hd + SUPER, hd), BF16), pltpu.VMEM((group, hd, hd), F32),
             pltpu.VMEM((group, SUPER, hd), F32), pltpu.VMEM((group, V7X_SUBLANES, hd), F32)]
    return pl.pallas_call(
        functools.partial(_dn_body, T=T, rt=rt, group=group),
        grid=(B, H),
        in_specs=[col(0), col(H), col(2 * H), col(3 * H),
                  pl.BlockSpec((None, V7X_SUBLANES, T), lambda b, h: (b, h, 0)),
                  cw(0), cw(H), cw(2 * H),
                  pl.BlockSpec((None, V7X_SUBLANES, 2), lambda b, h: (h, 0, 0)),
                  pl.BlockSpec((1, hd), lambda b, h: (0, 0))],
        out_specs=pl.BlockSpec((None, T, hd), lambda b, h: (b, 0, h)),
        out_shape=jax.ShapeDtypeStruct((B, T, DN_WIDTH), BF16),
        scratch_shapes=[pltpu.VMEM((3, T + 2 * V7X_SUBLANES, hd), F32), tok, tok, tok,
                        pltpu.VMEM((4, T, V7X_LANES), F32), pltpu.VMEM((V7X_SUBLANES, T), F32), tok,
                        pltpu.VMEM((hd, hd), F32), pltpu.VMEM((hd, hd), F32),
                        pltpu.VMEM((_NMASK, SUPER, SUPER), F32)] + slots + slots,
        compiler_params=_params(("arbitrary", "arbitrary")),
        name="deltanet",
    )(y_dn, y_dn, y_dn, y_dn, y_sr, conv_w, conv_w, conv_w, lr, nw)


def _attn_body(sink_ref, q_ref, k_ref, v_ref, o_ref, k2, v2, bias, *, T, tq, rt):
    i = pl.program_id(1)
    nb = T // WINDOW
    half = ATTN_HEAD_DIM
    lane = lax.broadcasted_iota(jnp.int32, (WINDOW, V7X_LANES), 1)
    scale = ATTN_HEAD_DIM ** -0.5
    assert scale == 2.0 ** -3
    q_scale = (jnp.where(lane < half, scale, 0.0).astype(BF16), jnp.where(lane < half, 0.0, scale).astype(BF16))

    @pl.when(i == 0)
    def _init():
        zeros = jnp.zeros((WINDOW, V7X_LANES), BF16)
        for kv in range(ATTN_KV_HEADS):
            for r0 in (0, T + WINDOW):
                k2[kv, r0:r0 + WINDOW, :] = zeros
                v2[kv, r0:r0 + WINDOW, 0:V7X_LANES] = zeros
                v2[kv, r0:r0 + WINDOW, V7X_LANES:2 * V7X_LANES] = jnp.ones((WINDOW, V7X_LANES), BF16)
        lane_r = lax.broadcasted_iota(jnp.int32, (rt, V7X_LANES), 1)

        def fill(t, _):
            src = pl.ds(pl.multiple_of(t * rt, rt), rt)
            dst = pl.ds(pl.multiple_of(t * rt, rt) + WINDOW, rt)
            kf = k_ref[src, :].astype(F32)
            vf = v_ref[src, :].astype(F32)
            for kv in range(ATTN_KV_HEADS):
                sel = (lane_r < half) if kv == 0 else (lane_r >= half)
                km = jnp.where(sel, kf, 0.0)
                vm = jnp.where(sel, vf, 0.0)
                k2[kv, dst, :] = (km + pltpu.roll(km, half, 1)).astype(BF16)
                v2[kv, dst, 0:V7X_LANES] = (vm + pltpu.roll(vm, half, 1)).astype(BF16)
                v2[kv, dst, V7X_LANES:2 * V7X_LANES] = jnp.ones((rt, V7X_LANES), BF16)
            return 0

        lax.fori_loop(0, T // rt, fill, 0)
        a = lax.broadcasted_iota(jnp.int32, (WINDOW, 3 * WINDOW), 0)
        j = lax.broadcasted_iota(jnp.int32, (WINDOW, 3 * WINDOW), 1)
        dist = jnp.abs(j - WINDOW - a)
        for h in range(ATTN_Q_HEADS):
            slope = 2.0 ** (-8.0 * (h + 1) / ATTN_Q_HEADS)
            bias[h] = jnp.where(dist <= WINDOW, -slope * dist.astype(F32), MASKED)

    jrow = lax.broadcasted_iota(jnp.int32, (1, 3 * WINDOW), 1)

    def window(qi):
        return pl.ds(pl.multiple_of((i * (tq // WINDOW) + qi) * WINDOW, WINDOW), 3 * WINDOW)

    def score(qi, h):
        c0 = (h // 2) * V7X_LANES
        qm = q_ref[qi * WINDOW:(qi + 1) * WINDOW, c0:c0 + V7X_LANES] * q_scale[h % 2]
        return _dot_nt(qm, k2[h // ATTN_GROUP, window(qi), :])

    def attend(qi, h, s, pos_bias):
        sink = sink_ref[h]
        s = s + bias[h] + pos_bias
        m = jnp.maximum(jnp.max(s, axis=-1, keepdims=True), sink)
        p = jnp.exp(s - m).astype(BF16)
        ov = _dot(p, v2[h // ATTN_GROUP, window(qi), :])
        return ov[:, :V7X_LANES] / (ov[:, V7X_LANES:] + jnp.exp(sink - m))

    nblk = tq // WINDOW
    scores = [score(0, h) for h in range(ATTN_Q_HEADS)]
    for qi in range(nblk):
        gi = i * nblk + qi
        first_valid = jnp.where(gi == 0, WINDOW, 0)
        end_valid = jnp.where(gi == nb - 1, 2 * WINDOW, 3 * WINDOW)
        outside = (jrow < first_valid) | (jrow >= end_valid)
        pos_bias = jnp.where(outside, MASKED, 0.0)
        tiles = []
        for h in range(ATTN_Q_HEADS):
            tiles.append(attend(qi, h, scores[h], pos_bias))
            if qi + 1 < nblk:
                scores[h] = score(qi + 1, h)
        for pair in range(ATTN_Q_HEADS // 2):
            c0 = pair * V7X_LANES
            o_ref[qi * WINDOW:(qi + 1) * WINDOW, c0:c0 + V7X_LANES] = \
                jnp.where(lane < half, tiles[2 * pair], tiles[2 * pair + 1]).astype(o_ref.dtype)


def _attention(y_at, sink, tq):
    B, T, _ = y_at.shape
    rt = min(256, T)
    return pl.pallas_call(
        functools.partial(_attn_body, T=T, tq=tq, rt=rt),
        grid=(B, T // tq),
        in_specs=[pl.BlockSpec(memory_space=pltpu.SMEM),
                  pl.BlockSpec((None, tq, ATTN_WIDTH), lambda b, i: (b, i, 0)),
                  pl.BlockSpec((None, T, KV_WIDTH), lambda b, i: (b, 0, ATTN_WIDTH // KV_WIDTH)),
                  pl.BlockSpec((None, T, KV_WIDTH), lambda b, i: (b, 0, ATTN_WIDTH // KV_WIDTH + 1))],
        out_specs=pl.BlockSpec((None, tq, ATTN_WIDTH), lambda b, i: (b, i, 0)),
        out_shape=jax.ShapeDtypeStruct((B, T, ATTN_WIDTH), BF16),
        scratch_shapes=[pltpu.VMEM((ATTN_KV_HEADS, T + 2 * WINDOW, V7X_LANES), BF16),
                        pltpu.VMEM((ATTN_KV_HEADS, T + 2 * WINDOW, 2 * V7X_LANES), BF16),
                        pltpu.VMEM((ATTN_Q_HEADS, WINDOW, 3 * WINDOW), F32)],
        compiler_params=_params(("arbitrary", "arbitrary")),
        name="window_attn",
    )(sink, y_at, y_at, y_at)


def _merge_body(x_ref, a_ref, b_ref, g_ref, pa_ref, pb_ref, wo_ref, gpost_ref, o_ref, *, d_model):
    parts = _row_parts(x_ref.shape[0])
    ya = [_dot(a_ref[r, :], pa_ref[...]) for r in parts]
    yb = [_dot(b_ref[r, :], pb_ref[...]) for r in parts]
    mix = []
    for r, yap, ybp in zip(parts, ya, yb):
        g = g_ref[r, :].astype(F32)
        merged = jax.nn.sigmoid(g[:, :d_model]) * yap + jax.nn.sigmoid(g[:, d_model:]) * ybp
        mix.append(_dot(merged.astype(BF16), wo_ref[...]))
    for r, mp in zip(parts, mix):
        o_ref[r, :] = x_ref[r, :] + _rms(mp, gpost_ref[...])


def _merge(x, out_a, out_b, y_g, pa, pb, wo, gpost, tm):
    B, T, D = x.shape
    tok = lambda n: pl.BlockSpec((None, tm, n), lambda b, i: (b, i, 0))
    return pl.pallas_call(
        functools.partial(_merge_body, d_model=D),
        grid=(B, T // tm),
        in_specs=[tok(D), tok(out_a.shape[-1]), tok(out_b.shape[-1]), tok(y_g.shape[-1]),
                  _resident(pa.shape), _resident(pb.shape), _resident(wo.shape), _resident((1, D))],
        out_specs=tok(D),
        out_shape=jax.ShapeDtypeStruct((B, T, D), F32),
        compiler_params=_params(("arbitrary", "arbitrary")),
        name="merge",
    )(x, out_a, out_b, y_g, pa, pb, wo, gpost)


def _split_w_in(w_in):
    D = w_in.shape[0]
    H = DN_HEADS
    o_small = 4 * DN_WIDTH
    o_attn = o_small + 4 * H
    o_gate = o_attn + ATTN_WIDTH + 2 * KV_WIDTH
    wdn = w_in[:, :o_small].astype(BF16)
    small = w_in[:, o_small:o_attn].reshape(D, 4, H)
    small = jnp.transpose(small, (0, 2, 1))
    wsr =jnp.pad(small, ((0, 0), (0, 0), (0, V7X_SUBLANES - 4))).reshape(D, H * V7X_SUBLANES).T.astype(BF16)
    wat = w_in[:, o_attn:o_gate].astype(BF16)
    wg = w_in[:, o_gate:].astype(BF16)
    return wdn, wsr, wat, wg


def _gate_consts(a_log, dt_bias):
    H = DN_HEADS
    both = jnp.stack([a_log, dt_bias], axis=0).astype(F32)
    per_head = jnp.transpose(both, (2, 1, 0))
    return jnp.zeros((H, V7X_SUBLANES, 2), F32).at[:, 2:4, :].set(per_head)


def _tile(n, pref):
    t = min(pref, n)
    assert n % t == 0
    return t


def _trunk(x, layers):
    T = x.shape[1]
    tm = _tile(T, DENSE_TILE_ROWS)
    tq = _tile(T, ATTN_TILE_ROWS)
    for p in layers:
        x = _ffn(x, p["f1_pre"], p["f1_post"], p["f1_w_in"], p["f1_w_out"], tm)
        y_dn, y_sr, y_at, y_g = _proj(x, p["m_pre"], p["wdn"], p["wsr"], p["wat"], p["wg"], tm)
        out_a = _deltanet(y_dn, y_sr, p["conv_w"], p["lr"], p["dn_norm_w"])
        out_b = _attention(y_at, p["sink"], tq)
        x = _merge(x, out_a, out_b, y_g, p["pa"], p["pb"], p["wo"], p["m_post"], tm)
        x = _ffn(x, p["f2_pre"], p["f2_post"], p["f2_w_in"], p["f2_w_out"], tm)
    return x


def kernel(x_prompt, x_sample, ffn1_norm_pre, ffn1_norm_post, ffn1_w_in, ffn1_w_out, mix_norm_pre, mix_norm_post, mix_w_in, dn_conv_w, dn_a_log, dn_dt_bias, dn_norm_w, attn_sink, w_branch_a, w_branch_b, mix_w_out, ffn2_norm_pre, ffn2_norm_post, ffn2_w_in, ffn2_w_out):
    depth = ffn1_w_in.shape[0]
    row = lambda v: v.reshape(1, -1).astype(F32)
    layers = []
    for l in range(depth):
        wdn, wsr, wat, wg = _split_w_in(mix_w_in[l])
        lr = _gate_consts(dn_a_log[l], dn_dt_bias[l])
        layers.append(dict(
            f1_pre=row(ffn1_norm_pre[l]), f1_post=row(ffn1_norm_post[l]),
            f1_w_in=ffn1_w_in[l].astype(BF16), f1_w_out=ffn1_w_out[l].astype(BF16),
            m_pre=row(mix_norm_pre[l]), m_post=row(mix_norm_post[l]),
            wdn=wdn, wsr=wsr, wat=wat, wg=wg,
            conv_w=dn_conv_w[l].astype(F32), lr=lr, dn_norm_w=row(dn_norm_w[l]),
            sink=attn_sink[l].astype(F32),
            pa=w_branch_a[l].astype(BF16), pb=w_branch_b[l].astype(BF16), wo=mix_w_out[l].astype(BF16),
            f2_pre=row(ffn2_norm_pre[l]), f2_post=row(ffn2_norm_post[l]),
            f2_w_in=ffn2_w_in[l].astype(BF16), f2_w_out=ffn2_w_out[l].astype(BF16),
        ))
    return (_trunk(x_prompt, layers), _trunk(x_sample, layers))
```
